```python
import math
import jax
import jax.numpy as jnp
from jax import lax
import numpy as np

D_MODEL = 2048
BATCH = 8
SEQ = 2048
DEPTH = 1
DEC_BATCH = 128
DEC_SEQ = 1
PAST_LEN = 16384
PAGE_SIZE = 128

DIFF_HEAD_DIM = 64
DIFF_V_DIM = 2 * DIFF_HEAD_DIM
DIFF_HEADS = (D_MODEL // 2) // DIFF_V_DIM
DIFF_KV_HEADS = 2
DIFF_GROUP = DIFF_HEADS // DIFF_KV_HEADS
DIFF_SCALE = DIFF_HEAD_DIM ** -0.5
MLA_NOPE_DIM = 128
MLA_ROPE_DIM = 64
MLA_V_DIM = 128
MLA_HEADS = (D_MODEL // 2) // MLA_V_DIM
MLA_Q_RANK = 512
MLA_KV_RANK = 256
MLA_SCALE = (MLA_NOPE_DIM + MLA_ROPE_DIM) ** -0.5
PEER_HEADS = 8
PEER_KEY_DIM = 256
PEER_N_KEYS = 128
PEER_N_EXPERTS = PEER_N_KEYS * PEER_N_KEYS
PEER_TOPK = 16
PEER_BLOCK = 64
ROPE_THETA = 10000.0
Q_BLOCK = 128
LN_EPS = 1e-5
RMS_EPS = 1e-6
DEEPNORM_ALPHA = (2 * DEPTH) ** 0.25
DEEPNORM_BETA = (8 * DEPTH) ** -0.25

DIFF_Q_COLS = DIFF_HEADS * 2 * DIFF_HEAD_DIM
DIFF_K_COLS = DIFF_KV_HEADS * 2 * DIFF_HEAD_DIM
DIFF_V_COLS = DIFF_KV_HEADS * DIFF_V_DIM
D_IN = DIFF_Q_COLS + DIFF_K_COLS + DIFF_V_COLS + MLA_Q_RANK + MLA_KV_RANK + MLA_ROPE_DIM
IN_SPLIT_POINTS = (
    DIFF_Q_COLS,
    DIFF_Q_COLS + DIFF_K_COLS,
    DIFF_Q_COLS + DIFF_K_COLS + DIFF_V_COLS,
    DIFF_Q_COLS + DIFF_K_COLS + DIFF_V_COLS + MLA_Q_RANK,
    DIFF_Q_COLS + DIFF_K_COLS + DIFF_V_COLS + MLA_Q_RANK + MLA_KV_RANK,
)
MIX_OUT = DIFF_HEADS * DIFF_V_DIM + MLA_HEADS * MLA_V_DIM

kernel_name = 'hymba_diffattn_mla_peer_step'


def _rms_norm(x, g):
    xf = x.astype(jnp.float32)
    y = xf * lax.rsqrt(jnp.mean(xf * xf, axis=-1, keepdims=True) + RMS_EPS) * g.astype(jnp.float32)
    return y.astype(x.dtype)


def _layer_norm(x, g, b):
    xf = x.astype(jnp.float32)
    xc = xf - jnp.mean(xf, axis=-1, keepdims=True)
    var = jnp.mean(xc * xc, axis=-1, keepdims=True)
    y = xc * lax.rsqrt(var + LN_EPS) * g.astype(jnp.float32) + b.astype(jnp.float32)
    return y.astype(x.dtype)


def _rope(x, pos):
    half = x.shape[-1] // 2
    inv_freq = ROPE_THETA ** (-jnp.arange(half, dtype=jnp.float32) / half)
    ang = pos.astype(jnp.float32)[:, None] * inv_freq[None, :]
    shape = (1, pos.shape[0]) + (1,) * (x.ndim - 3) + (half,)
    cos = jnp.cos(ang).reshape(shape)
    sin = jnp.sin(ang).reshape(shape)
    xf = x.astype(jnp.float32)
    x1, x2 = xf[..., :half], xf[..., half:]
    return jnp.concatenate([x1 * cos - x2 * sin, x2 * cos + x1 * sin], axis=-1).astype(x.dtype)


def _split_blocks(a):
    b, s = a.shape[0], a.shape[1]
    return a.reshape((b, s // Q_BLOCK, Q_BLOCK) + a.shape[2:]).swapaxes(0, 1)


def _merge_blocks(a):
    a = a.swapaxes(0, 1)
    return a.reshape((a.shape[0], a.shape[1] * a.shape[2]) + a.shape[3:])


def _online_init(s, pv):
    m = jnp.max(s, axis=-1)
    p = jnp.exp(s - m[..., None])
    return (m, jnp.sum(p, axis=-1), pv(p))


def _online_merge(carry, s, pv):
    m, l, acc = carry
    m_new = jnp.maximum(m, jnp.max(s, axis=-1))
    corr = jnp.exp(m - m_new)
    p = jnp.exp(s - m_new[..., None])
    return (m_new, l * corr + jnp.sum(p, axis=-1), acc * corr[..., None] + pv(p))


def _project(x, pos, w_in, q_norm_g, w_uq, kv_norm_g, w_ukv):
    b, s, _ = x.shape
    z = x @ w_in
    qd, kd, vd, cq, ckv, kr = jnp.split(z, IN_SPLIT_POINTS, axis=-1)
    qd = _rope(qd.reshape(b, s, DIFF_KV_HEADS, DIFF_GROUP, 2, DIFF_HEAD_DIM), pos)
    kd = _rope(kd.reshape(b, s, DIFF_KV_HEADS, 2, DIFF_HEAD_DIM), pos)
    vd = vd.reshape(b, s, DIFF_KV_HEADS, DIFF_V_DIM)
    q = jnp.einsum('bsr,rhe->bshe', _rms_norm(cq, q_norm_g), w_uq)
    q_rope = _rope(q[..., MLA_NOPE_DIM:], pos)
    q_lat = jnp.einsum('bshn,rhn->bshr', q[..., :MLA_NOPE_DIM], w_ukv[..., :MLA_NOPE_DIM])
    ckv = _rms_norm(ckv, kv_norm_g)
    kr = _rope(kr, pos)
    return qd, kd, vd, q_lat, q_rope, ckv, kr


def _diff_attn_prompt(q, k, v, lam):
    seq = q.shape[1]
    kpos = jnp.arange(seq)

    def block(args):
        qb, start = args
        s = jnp.einsum('bqhgcd,bshcd->bchgqs', qb, k, preferred_element_type=jnp.float32) * DIFF_SCALE
        qpos = start + jnp.arange(Q_BLOCK)
        s = jnp.where(kpos[None, :] <= qpos[:, None], s, -jnp.inf)
        p = jax.nn.softmax(s, axis=-1)
        return jnp.einsum('bchgqs,bshv->bqhgcv', p, v, preferred_element_type=jnp.float32)

    o = _merge_blocks(lax.map(block, (_split_blocks(q), jnp.arange(seq // Q_BLOCK) * Q_BLOCK)))
    return o[..., 0, :] - lam * o[..., 1, :]


def _diff_attn_sample(q, k_new, v_new, cache_k, cache_v, page_table, layer, lam):
    t = q.shape[1]
    causal = jnp.arange(t)[None, :] <= jnp.arange(t)[:, None]

    def scores(kb):
        return jnp.einsum('bqhgcd,bshcd->bchgqs', q, kb, preferred_element_type=jnp.float32) * DIFF_SCALE

    def pv_of(vb):
        return lambda p: jnp.einsum('bchgqs,bshv->bchgqv', p, vb, preferred_element_type=jnp.float32)

    carry = _online_init(jnp.where(causal, scores(k_new), -jnp.inf), pv_of(v_new))

    def step(carry, pages):
        kb = cache_k[layer, pages]
        vb = cache_v[layer, pages]
        return _online_merge(carry, scores(kb), pv_of(vb)), None

    (m, l, acc), _ = lax.scan(step, carry, page_table.T)
    o = (acc / l[..., None]).transpose(0, 4, 2, 3, 1, 5)
    return o[..., 0, :] - lam * o[..., 1, :]


def _mla_attn_prompt(q_lat, q_rope, ckv, kr):
    seq = q_lat.shape[1]
    kpos = jnp.arange(seq)

    def block(args):
        qlb, qrb, start = args
        s = (jnp.einsum('bqhr,bsr->bhqs', qlb, ckv, preferred_element_type=jnp.float32)
             + jnp.einsum('bqhe,bse->bhqs', qrb, kr, preferred_element_type=jnp.float32)) * MLA_SCALE
        qpos = start + jnp.arange(Q_BLOCK)
        s = jnp.where(kpos[None, :] <= qpos[:, None], s, -jnp.inf)
        p = jax.nn.softmax(s, axis=-1)
        return jnp.einsum('bhqs,bsr->bqhr', p, ckv, preferred_element_type=jnp.float32)

    xs = (_split_blocks(q_lat), _split_blocks(q_rope), jnp.arange(seq // Q_BLOCK) * Q_BLOCK)
    return _merge_blocks(lax.map(block, xs))


def _mla_attn_sample(q_lat, q_rope, ckv_new, kr_new, cache_ckv, cache_kr, page_table, layer):
    t = q_lat.shape[1]
    causal = jnp.arange(t)[None, :] <= jnp.arange(t)[:, None]

    def scores(cb, rb):
        return (jnp.einsum('bqhr,bsr->bhqs', q_lat, cb, preferred_element_type=jnp.float32)
                + jnp.einsum('bqhe,bse->bhqs', q_rope, rb, preferred_element_type=jnp.float32)) * MLA_SCALE

    def pv_of(cb):
        return lambda p: jnp.einsum('bhqs,bsr->bhqr', p, cb, preferred_element_type=jnp.float32)

    carry = _online_init(jnp.where(causal, scores(ckv_new, kr_new), -jnp.inf), pv_of(ckv_new))

    def step(carry, pages):
        cb = cache_ckv[layer, pages]
        rb = cache_kr[layer, pages]
        return _online_merge(carry, scores(cb, rb), pv_of(cb)), None

    (m, l, acc), _ = lax.scan(step, carry, page_table.T)
    return (acc / l[..., None]).transpose(0, 2, 1, 3)


def _peer(x2d, wq, q_norm_g, subkeys, u_tab, v_tab):
    t, d = x2d.shape
    pad = (-t) % PEER_BLOCK
    xb = jnp.pad(x2d, ((0, pad), (0, 0))).reshape(-1, PEER_BLOCK, d)

    def block(xt):
        q = _rms_norm((xt @ wq).reshape(PEER_BLOCK, PEER_HEADS, PEER_KEY_DIM), q_norm_g)
        q = q.reshape(PEER_BLOCK, PEER_HEADS, 2, PEER_KEY_DIM // 2)
        s = jnp.einsum('thcd,hcnd->thcn', q, subkeys, preferred_element_type=jnp.float32)
        s1, i1 = lax.top_k(s[:, :, 0], PEER_TOPK)
        s2, i2 = lax.top_k(s[:, :, 1], PEER_TOPK)
        cand = (s1[..., :, None] + s2[..., None, :]).reshape(PEER_BLOCK, PEER_HEADS, PEER_TOPK * PEER_TOPK)
        sc, ci = lax.top_k(cand, PEER_TOPK)
        e = (jnp.take_along_axis(i1, ci // PEER_TOPK, axis=-1) * PEER_N_KEYS
             + jnp.take_along_axis(i2, ci % PEER_TOPK, axis=-1)).reshape(PEER_BLOCK, -1)
        g = jax.nn.softmax(sc, axis=-1).reshape(PEER_BLOCK, -1)
        h = jax.nn.gelu(jnp.einsum('td,tkd->tk', xt, u_tab[e], preferred_element_type=jnp.float32),
                        approximate=False)
        return jnp.einsum('tk,tkd->td', (g * h).astype(xt.dtype), v_tab[e])

    return lax.map(block, xb).reshape(-1, d)[:t]


def _block_out(x, od, om, lam_init, subln_g, w_ukv, w_out, ln1_g, ln1_b, ln2_g, ln2_b,
               peer_wq, peer_q_norm_g, peer_subkeys, peer_u, peer_v):
    b, s, _ = x.shape
    od = _rms_norm(od, subln_g) * (1.0 - lam_init)
    om = jnp.einsum('bshr,rhv->bshv', om, w_ukv[..., MLA_NOPE_DIM:])
    mix = jnp.concatenate([od.reshape(b, s, -1), om.reshape(b, s, -1)], axis=-1).astype(x.dtype)
    x1 = _layer_norm(DEEPNORM_ALPHA * x + mix @ w_out, ln1_g, ln1_b)
    f = _peer(x1.reshape(b * s, -1), peer_wq, peer_q_norm_g, peer_subkeys, peer_u, peer_v).reshape(b, s, -1)
    return _layer_norm(DEEPNORM_ALPHA * x1 + f, ln2_g, ln2_b)


def setup_inputs(seed: int = 0) -> dict:
    key = jax.random.key(seed)
    ks = jax.random.split(key, 32)
    f32 = jnp.float32
    n_pages = PAST_LEN // PAGE_SIZE
    n_pool = (DEC_BATCH * n_pages * 5) // 4

    def nrm(k, shape, scale):
        return jax.random.normal(k, shape, f32) * scale

    x_prompt = nrm(ks[0], (BATCH, SEQ, D_MODEL), 1.0)
    x_sample = nrm(ks[1], (DEC_BATCH, DEC_SEQ, D_MODEL), 1.0)
    cache_diff_k = nrm(ks[2], (DEPTH, n_pool, PAGE_SIZE, DIFF_KV_HEADS, 2, DIFF_HEAD_DIM), 1.0)
    cache_diff_v = nrm(ks[3], (DEPTH, n_pool, PAGE_SIZE, DIFF_KV_HEADS, DIFF_V_DIM), DEEPNORM_BETA)
    cache_mla_ckv = nrm(ks[4], (DEPTH, n_pool, PAGE_SIZE, MLA_KV_RANK), 1.0)
    cache_mla_krope = nrm(ks[5], (DEPTH, n_pool, PAGE_SIZE, MLA_ROPE_DIM), 1.0)
    page_table = jax.random.permutation(ks[6], n_pool)[: DEC_BATCH * n_pages].reshape(
        DEC_BATCH, n_pages).astype(jnp.int32)
    col_scale = jnp.concatenate([
        jnp.ones((DIFF_Q_COLS + DIFF_K_COLS,), f32),
        jnp.full((DIFF_V_COLS,), DEEPNORM_BETA, f32),
        jnp.ones((MLA_Q_RANK + MLA_KV_RANK + MLA_ROPE_DIM,), f32)])
    w_in = nrm(ks[7], (DEPTH, D_MODEL, D_IN), D_MODEL ** -0.5) * col_scale
    diff_lambda_q1 = nrm(ks[8], (DEPTH, DIFF_HEAD_DIM), 0.1)
    diff_lambda_k1 = nrm(ks[9], (DEPTH, DIFF_HEAD_DIM), 0.1)
    diff_lambda_q2 = nrm(ks[10], (DEPTH, DIFF_HEAD_DIM), 0.1)
    diff_lambda_k2 = nrm(ks[11], (DEPTH, DIFF_HEAD_DIM), 0.1)
    diff_subln_g = 1.0 + nrm(ks[12], (DEPTH, DIFF_V_DIM), 0.02)
    mla_q_norm_g = 1.0 + nrm(ks[13], (DEPTH, MLA_Q_RANK), 0.02)
    mla_kv_norm_g = 1.0 + nrm(ks[14], (DEPTH, MLA_KV_RANK), 0.02)
    w_uq = nrm(ks[15], (DEPTH, MLA_Q_RANK, MLA_HEADS, MLA_NOPE_DIM + MLA_ROPE_DIM), MLA_Q_RANK ** -0.5)
    uv_scale = jnp.concatenate([jnp.ones((MLA_NOPE_DIM,), f32), jnp.full((MLA_V_DIM,), DEEPNORM_BETA, f32)])
    w_ukv = nrm(ks[16], (DEPTH, MLA_KV_RANK, MLA_HEADS, MLA_NOPE_DIM + MLA_V_DIM), MLA_KV_RANK ** -0.5) * uv_scale
    w_out = nrm(ks[17], (DEPTH, MIX_OUT, D_MODEL), MIX_OUT ** -0.5 * DEEPNORM_BETA)
    ln1_g = 1.0 + nrm(ks[18], (DEPTH, D_MODEL), 0.02)
    ln1_b = nrm(ks[19], (DEPTH, D_MODEL), 0.02)
    ln2_g = 1.0 + nrm(ks[20], (DEPTH, D_MODEL), 0.02)
    ln2_b = nrm(ks[21], (DEPTH, D_MODEL), 0.02)
    peer_wq = nrm(ks[22], (DEPTH, D_MODEL, PEER_HEADS * PEER_KEY_DIM), D_MODEL ** -0.5)
    peer_q_norm_g = 1.0 + nrm(ks[23], (DEPTH, PEER_HEADS, PEER_KEY_DIM), 0.02)
    peer_subkeys = nrm(ks[24], (DEPTH, PEER_HEADS, 2, PEER_N_KEYS, PEER_KEY_DIM // 2), (PEER_KEY_DIM // 2) ** -0.5)
    peer_u = nrm(ks[25], (DEPTH, PEER_N_EXPERTS, D_MODEL), D_MODEL ** -0.5)
    peer_v = nrm(ks[26], (DEPTH, PEER_N_EXPERTS, D_MODEL), DEEPNORM_BETA)
    return {
        'x_prompt': x_prompt, 'x_sample': x_sample,
        'cache_diff_k': cache_diff_k, 'cache_diff_v': cache_diff_v,
        'cache_mla_ckv': cache_mla_ckv, 'cache_mla_krope': cache_mla_krope,
        'page_table': page_table,
        'w_in': w_in,
        'diff_lambda_q1': diff_lambda_q1, 'diff_lambda_k1': diff_lambda_k1,
        'diff_lambda_q2': diff_lambda_q2, 'diff_lambda_k2': diff_lambda_k2,
        'diff_subln_g': diff_subln_g,
        'mla_q_norm_g': mla_q_norm_g, 'mla_kv_norm_g': mla_kv_norm_g,
        'w_uq': w_uq, 'w_ukv': w_ukv, 'w_out': w_out,
        'ln1_g': ln1_g, 'ln1_b': ln1_b, 'ln2_g': ln2_g, 'ln2_b': ln2_b,
        'peer_wq': peer_wq, 'peer_q_norm_g': peer_q_norm_g, 'peer_subkeys': peer_subkeys,
        'peer_u': peer_u, 'peer_v': peer_v,
    }


def reference(x_prompt, x_sample, cache_diff_k, cache_diff_v, cache_mla_ckv, cache_mla_krope, page_table,
              w_in, diff_lambda_q1, diff_lambda_k1, diff_lambda_q2, diff_lambda_k2, diff_subln_g,
              mla_q_norm_g, mla_kv_norm_g, w_uq, w_ukv, w_out, ln1_g, ln1_b, ln2_g, ln2_b,
              peer_wq, peer_q_norm_g, peer_subkeys, peer_u, peer_v):
    f32 = jnp.float32
    pos_p = jnp.arange(x_prompt.shape[1], dtype=jnp.int32)
    pos_s = PAST_LEN + jnp.arange(x_sample.shape[1], dtype=jnp.int32)
    xp, xs = x_prompt, x_sample
    p_dk, p_dv, p_ckv, p_kr = [], [], [], []
    s_dk, s_dv, s_ckv, s_kr = [], [], [], []
    for layer in range(DEPTH):
        lam_init = 0.8 - 0.6 * math.exp(-0.3 * layer)
        lam = (jnp.exp(jnp.sum(diff_lambda_q1[layer].astype(f32) * diff_lambda_k1[layer].astype(f32)))
               - jnp.exp(jnp.sum(diff_lambda_q2[layer].astype(f32) * diff_lambda_k2[layer].astype(f32)))
               + lam_init)
        proj_w = (w_in[layer], mla_q_norm_g[layer], w_uq[layer], mla_kv_norm_g[layer], w_ukv[layer])
        out_w = (lam_init, diff_subln_g[layer], w_ukv[layer], w_out[layer],
                 ln1_g[layer], ln1_b[layer], ln2_g[layer], ln2_b[layer],
                 peer_wq[layer], peer_q_norm_g[layer], peer_subkeys[layer], peer_u[layer], peer_v[layer])

        qd, kd, vd, ql, qr, ckv, kr = _project(xp, pos_p, *proj_w)
        od = _diff_attn_prompt(qd, kd, vd, lam)
        om = _mla_attn_prompt(ql, qr, ckv, kr)
        p_dk.append(kd)
        p_dv.append(vd)
        p_ckv.append(ckv)
        p_kr.append(kr)
        xp = _block_out(xp, od, om, *out_w)

        qd, kd, vd, ql, qr, ckv, kr = _project(xs, pos_s, *proj_w)
        od = _diff_attn_sample(qd, kd, vd, cache_diff_k, cache_diff_v, page_table, layer, lam)
        om = _mla_attn_sample(ql, qr, ckv, kr, cache_mla_ckv, cache_mla_krope, page_table, layer)
        s_dk.append(kd)
        s_dv.append(vd)
        s_ckv.append(ckv)
        s_kr.append(kr)
        xs = _block_out(xs, od, om, *out_w)

    return (xp, xs,
            jnp.stack(p_dk), jnp.stack(p_dv), jnp.stack(p_ckv), jnp.stack(p_kr),
            jnp.stack(s_dk), jnp.stack(s_dv), jnp.stack(s_ckv), jnp.stack(s_kr))
```

```python
import functools
import math

import jax
import jax.numpy as jnp
from jax import lax
from jax.experimental import pallas as pl
from jax.experimental.pallas import tpu as pltpu

BF16 = jnp.bfloat16
F32 = jnp.float32

D_MODEL = 2048
DIFF_HEAD_DIM = 64
DIFF_V_DIM = 128
DIFF_KV_HEADS = 2
DIFF_GROUP = 4
DIFF_SCALE = DIFF_HEAD_DIM ** -0.5
MLA_NOPE_DIM = 128
MLA_ROPE_DIM = 64
MLA_V_DIM = 128
MLA_HEADS = 8
MLA_Q_RANK = 512
MLA_KV_RANK = 256
MLA_SCALE = (MLA_NOPE_DIM + MLA_ROPE_DIM) ** -0.5
PEER_HEADS = 8
PEER_KEY_DIM = 256
PEER_N_KEYS = 128
PEER_TOPK = 16
ROPE_THETA = 10000.0
LN_EPS = 1e-5
RMS_EPS = 1e-6

DIFF_Q_COLS = 1024
DIFF_K_COLS = 256
DIFF_V_COLS = 256
OFF_K = DIFF_Q_COLS
OFF_V = OFF_K + DIFF_K_COLS
OFF_CQ = OFF_V + DIFF_V_COLS
OFF_CKV = OFF_CQ + MLA_Q_RANK
OFF_KR = OFF_CKV + MLA_KV_RANK
D_IN_PAD = OFF_KR + 128

LANES = 128
VMEM_LIMIT = 56 * 1024 * 1024

NT_DIMS = (((1,), (1,)), ((), ()))


def _cparams(sem):
    return pltpu.CompilerParams(dimension_semantics=sem, vmem_limit_bytes=VMEM_LIMIT)


def _rope_chunk(c, cos, sin_signed, first_half):
    sw = jnp.where(first_half, pltpu.roll(c, 96, 1), pltpu.roll(c, 32, 1))
    return c * cos + sw * sin_signed


def _proj_kernel(x_ref, cos_ref, sin_ref, win_ref, qg_ref, wuq_ref, kvg_ref, wn_ref,
                 qd_ref, kd_ref, vd_ref, ckv_ref, kr_ref, kvd_ref, kcat_ref, qlat_ref, qrope_ref):
    tm = x_ref.shape[0]
    z = jnp.dot(x_ref[...].astype(BF16), win_ref[...], preferred_element_type=F32)
    cos = cos_ref[...]
    sin = sin_ref[...]
    lane = lax.broadcasted_iota(jnp.int32, (tm, LANES), 1)
    first = (lane % 64) < 32

    def rope(c):
        return _rope_chunk(c, cos, sin, first)

    for j in range(DIFF_Q_COLS // LANES):
        qd_ref[:, j * LANES:(j + 1) * LANES] = rope(z[:, j * LANES:(j + 1) * LANES]).astype(BF16)
    for j in range(DIFF_K_COLS // LANES):
        k = rope(z[:, OFF_K + j * LANES:OFF_K + (j + 1) * LANES])
        kd_ref[:, j * LANES:(j + 1) * LANES] = k
        kvd_ref[:, j * LANES:(j + 1) * LANES] = k.astype(BF16)
    v = z[:, OFF_V:OFF_V + DIFF_V_COLS]
    vd_ref[...] = v
    kvd_ref[:, DIFF_K_COLS:] = v.astype(BF16)

    ckv = z[:, OFF_CKV:OFF_CKV + MLA_KV_RANK]
    ckv = ckv * lax.rsqrt(jnp.mean(ckv * ckv, axis=-1, keepdims=True) + RMS_EPS) * kvg_ref[...]
    ckv_ref[...] = ckv
    kcat_ref[:, :MLA_KV_RANK] = ckv.astype(BF16)
    kr2 = rope(z[:, OFF_KR:OFF_KR + LANES])
    kr_ref[...] = kr2[:, :MLA_ROPE_DIM]
    kcat_ref[:, MLA_KV_RANK:] = kr2.astype(BF16)

    cq = z[:, OFF_CQ:OFF_CQ + MLA_Q_RANK]
    cq = cq * lax.rsqrt(jnp.mean(cq * cq, axis=-1, keepdims=True) + RMS_EPS) * qg_ref[...]
    q = jnp.dot(cq.astype(BF16), wuq_ref[...], preferred_element_type=F32)
    nope_cols = MLA_HEADS * MLA_NOPE_DIM
    for j in range(MLA_HEADS * MLA_ROPE_DIM // LANES):
        qrope_ref[:, j * LANES:(j + 1) * LANES] = rope(
            q[:, nope_cols + j * LANES:nope_cols + (j + 1) * LANES]).astype(BF16)
    for h in range(MLA_HEADS):
        qn = q[:, h * MLA_NOPE_DIM:(h + 1) * MLA_NOPE_DIM].astype(BF16)
        qlat_ref[:, h * MLA_KV_RANK:(h + 1) * MLA_KV_RANK] = jnp.dot(
            qn, wn_ref[h], preferred_element_type=F32).astype(BF16)


def _proj(x2d, cos, sin, w_in_p, q_norm_g, w_uq_p, kv_norm_g, wn, tm):
    t = x2d.shape[0]
    row = lambda w: pl.BlockSpec((tm, w), lambda i: (i, 0))
    full = lambda a: pl.BlockSpec(a.shape, lambda i: (0,) * a.ndim)
    outs = [
        (DIFF_Q_COLS, BF16), (DIFF_K_COLS, F32), (DIFF_V_COLS, F32), (MLA_KV_RANK, F32),
        (MLA_ROPE_DIM, F32), (DIFF_K_COLS + DIFF_V_COLS, BF16), (MLA_KV_RANK + LANES, BF16),
        (MLA_HEADS * MLA_KV_RANK, BF16), (MLA_HEADS * MLA_ROPE_DIM, BF16),
    ]
    return pl.pallas_call(
        _proj_kernel,
        grid=(t // tm,),
        in_specs=[row(D_MODEL), row(LANES), row(LANES), full(w_in_p), full(q_norm_g), full(w_uq_p),
                  full(kv_norm_g), full(wn)],
        out_specs=[row(w) for w, _ in outs],
        out_shape=[jax.ShapeDtypeStruct((t, w), dt) for w, dt in outs],
        compiler_params=_cparams(("parallel",)),
    )(x2d, cos, sin, w_in_p, q_norm_g, w_uq_p, kv_norm_g, wn)


def _online_update(s, v, m_s, l_s, acc_s):
    m_prev = m_s[...]
    m_new = jnp.maximum(m_prev, jnp.max(s, axis=-1, keepdims=True))
    corr = jnp.exp(m_prev - m_new)
    p = jnp.exp(s - m_new)
    l_s[...] = l_s[...] * corr + jnp.sum(p, axis=-1, keepdims=True)
    acc_s[...] = acc_s[...] * corr + jnp.dot(p.astype(BF16), v, preferred_element_type=F32)
    m_s[...] = m_new


def _causal_mask(s, tq):
    rows = lax.broadcasted_iota(jnp.int32, s.shape, 0) & (tq - 1)
    cols = lax.broadcasted_iota(jnp.int32, s.shape, 1)
    return jnp.where(cols <= rows, s, -jnp.inf)


def _diff_flash_kernel(lam_ref, q_ref, k_ref, v_ref, g_ref, o_ref, qs, m_s, l_s, acc_s, *, tq, lam_init):
    qi = pl.program_id(2)
    kj = pl.program_id(3)
    ngc = 2 * DIFF_GROUP

    @pl.when(kj == 0)
    def _():
        lane = lax.broadcasted_iota(jnp.int32, (tq, LANES), 1)
        zero = jnp.zeros((tq, LANES), BF16)
        for g in range(DIFF_GROUP):
            chunk = q_ref[:, g * LANES:(g + 1) * LANES]
            qs[g * tq:(g + 1) * tq, :] = jnp.where(lane < 64, chunk, zero)
            qs[(DIFF_GROUP + g) * tq:(DIFF_GROUP + g + 1) * tq, :] = jnp.where(lane >= 64, chunk, zero)
        m_s[...] = jnp.full(m_s.shape, -jnp.inf, F32)
        l_s[...] = jnp.zeros(l_s.shape, F32)
        acc_s[...] = jnp.zeros(acc_s.shape, F32)

    def scores():
        return lax.dot_general(qs[...], k_ref[...], NT_DIMS, preferred_element_type=F32) * DIFF_SCALE

    @pl.when(kj < qi)
    def _():
        _online_update(scores(), v_ref[...], m_s, l_s, acc_s)

    @pl.when(kj == qi)
    def _():
        _online_update(_causal_mask(scores(), tq), v_ref[...], m_s, l_s, acc_s)

    @pl.when(kj == pl.num_programs(3) - 1)
    def _():
        o = acc_s[...] / l_s[...]
        d = o[:DIFF_GROUP * tq] - lam_ref[0] * o[DIFF_GROUP * tq:]
        d = d * lax.rsqrt(jnp.mean(d * d, axis=-1, keepdims=True) + RMS_EPS) * g_ref[...] * (1.0 - lam_init)
        for g in range(DIFF_GROUP):
            o_ref[:, g * DIFF_V_DIM:(g + 1) * DIFF_V_DIM] = d[g * tq:(g + 1) * tq].astype(BF16)


def _diff_flash(lam, qd, kvd, subln_g, batch, seq, tq, lam_init):
    nq = seq // tq
    ngc = 2 * DIFF_GROUP
    kern = functools.partial(_diff_flash_kernel, tq=tq, lam_init=lam_init)
    gw = DIFF_GROUP * 2 * DIFF_HEAD_DIM
    return pl.pallas_call(
        kern,
        grid=(batch, DIFF_KV_HEADS, nq, nq),
        in_specs=[
            pl.BlockSpec(memory_space=pltpu.SMEM),
            pl.BlockSpec((tq, gw), lambda b, h, i, j: (b * nq + i, h)),
            pl.BlockSpec((tq, LANES), lambda b, h, i, j: (b * nq + jnp.minimum(i, j), h)),
            pl.BlockSpec((tq, LANES), lambda b, h, i, j: (b * nq + jnp.minimum(i, j), DIFF_KV_HEADS + h)),
            pl.BlockSpec((1, DIFF_V_DIM), lambda b, h, i, j: (0, 0)),
        ],
        out_specs=pl.BlockSpec((tq, DIFF_GROUP * DIFF_V_DIM), lambda b, h, i, j: (b * nq + i, h)),
        out_shape=jax.ShapeDtypeStruct((batch * seq, DIFF_KV_HEADS * DIFF_GROUP * DIFF_V_DIM), BF16),
        scratch_shapes=[
            pltpu.VMEM((ngc * tq, LANES), BF16),
            pltpu.VMEM((ngc * tq, 1), F32),
            pltpu.VMEM((ngc * tq, 1), F32),
            pltpu.VMEM((ngc * tq, DIFF_V_DIM), F32),
        ],
        compiler_params=_cparams(("parallel", "parallel", "parallel", "arbitrary")),
    )(lam, qd, kvd, kvd, subln_g)


def _mla_flash_kernel(ql_ref, qr_ref, k_ref, o_ref, qs, m_s, l_s, acc_s, *, tq):
    qi = pl.program_id(1)
    kj = pl.program_id(2)

    @pl.when(kj == 0)
    def _():
        lane = lax.broadcasted_iota(jnp.int32, (tq, LANES), 1)
        zero = jnp.zeros((tq, LANES), BF16)
        for h in range(MLA_HEADS):
            qs[h * tq:(h + 1) * tq, :MLA_KV_RANK] = ql_ref[:, h * MLA_KV_RANK:(h + 1) * MLA_KV_RANK]
            chunk = qr_ref[:, (h // 2) * LANES:(h // 2 + 1) * LANES]
            keep = (lane < 64) if h % 2 == 0 else (lane >= 64)
            qs[h * tq:(h + 1) * tq, MLA_KV_RANK:] = jnp.where(keep, chunk, zero)
        m_s[...] = jnp.full(m_s.shape, -jnp.inf, F32)
        l_s[...] = jnp.zeros(l_s.shape, F32)
        acc_s[...] = jnp.zeros(acc_s.shape, F32)

    def scores():
        return lax.dot_general(qs[...], k_ref[...], NT_DIMS, preferred_element_type=F32) * MLA_SCALE

    @pl.when(kj < qi)
    def _():
        _online_update(scores(), k_ref[:, :MLA_KV_RANK], m_s, l_s, acc_s)

    @pl.when(kj == qi)
    def _():
        _online_update(_causal_mask(scores(), tq), k_ref[:, :MLA_KV_RANK], m_s, l_s, acc_s)

    @pl.when(kj == pl.num_programs(2) - 1)
    def _():
        o = acc_s[...] / l_s[...]
        for h in range(MLA_HEADS):
            o_ref[:, h * MLA_KV_RANK:(h + 1) * MLA_KV_RANK] = o[h * tq:(h + 1) * tq].astype(BF16)


def _mla_flash(qlat, qrope, kcat, batch, seq, tq):
    nq = seq // tq
    kern = functools.partial(_mla_flash_kernel, tq=tq)
    kw = MLA_KV_RANK + LANES
    return pl.pallas_call(
        kern,
        grid=(batch, nq, nq),
        in_specs=[
            pl.BlockSpec((tq, MLA_HEADS * MLA_KV_RANK), lambda b, i, j: (b * nq + i, 0)),
            pl.BlockSpec((tq, MLA_HEADS * MLA_ROPE_DIM), lambda b, i, j: (b * nq + i, 0)),
            pl.BlockSpec((tq, kw), lambda b, i, j: (b * nq + jnp.minimum(i, j), 0)),
        ],
        out_specs=pl.BlockSpec((tq, MLA_HEADS * MLA_KV_RANK), lambda b, i, j: (b * nq + i, 0)),
        out_shape=jax.ShapeDtypeStruct((batch * seq, MLA_HEADS * MLA_KV_RANK), BF16),
        scratch_shapes=[
            pltpu.VMEM((MLA_HEADS * tq, kw), BF16),
            pltpu.VMEM((MLA_HEADS * tq, 1), F32),
            pltpu.VMEM((MLA_HEADS * tq, 1), F32),
            pltpu.VMEM((MLA_HEADS * tq, MLA_KV_RANK), F32),
        ],
        compiler_params=_cparams(("parallel", "parallel", "arbitrary")),
    )(qlat, qrope, kcat)


def _decode_kernel(pt_ref, lam_ref, qd_ref, kdn_ref, vdn_ref, ql_ref, qr_ref, ckvn_ref, krn_ref, g_ref, *rest,
                   pps, lam_init):
    dk_refs = rest[0 * pps:1 * pps]
    dv_refs = rest[1 * pps:2 * pps]
    ck_refs = rest[2 * pps:3 * pps]
    kr_refs = rest[3 * pps:4 * pps]
    od_ref, om_ref = rest[4 * pps:4 * pps + 2]
    md_s, ld_s, accd_s, mm_s, lm_s, accm_s = rest[4 * pps + 2:]
    step = pl.program_id(1)

    @pl.when(step == 0)
    def _():
        for h in range(DIFF_KV_HEADS):
            q = qd_ref[0, h].astype(F32)
            kn = kdn_ref[0, :, h * LANES:(h + 1) * LANES]
            md_s[h] = jnp.sum(q * kn.astype(BF16).astype(F32), axis=-1, keepdims=True) * DIFF_SCALE
            ld_s[h] = jnp.ones((8, 1), F32)
            accd_s[h] = jnp.broadcast_to(vdn_ref[0, :, h * LANES:(h + 1) * LANES].astype(BF16).astype(F32),
                                         (8, DIFF_V_DIM))
        ql = ql_ref[0].astype(F32)
        qr = qr_ref[0].astype(F32)
        cn = ckvn_ref[0].astype(BF16).astype(F32)
        rn = krn_ref[0].astype(BF16).astype(F32)
        mm_s[...] = (jnp.sum(ql * cn, axis=-1, keepdims=True)
                     + jnp.sum(qr * rn, axis=-1, keepdims=True)) * MLA_SCALE
        lm_s[...] = jnp.ones((8, 1), F32)
        accm_s[...] = jnp.broadcast_to(cn, (8, MLA_KV_RANK))

    def merge(s_list, v_list, m_ref, l_ref, acc_ref):
        s = jnp.concatenate(s_list, axis=-1)
        m_prev = m_ref[...]
        m_new = jnp.maximum(m_prev, jnp.max(s, axis=-1, keepdims=True))
        corr = jnp.exp(m_prev - m_new)
        p = jnp.exp(s - m_new)
        l_ref[...] = l_ref[...] * corr + jnp.sum(p, axis=-1, keepdims=True)
        pb = p.astype(BF16)
        acc = acc_ref[...] * corr
        for i, v in enumerate(v_list):
            acc = acc + jnp.dot(pb[:, i * LANES:(i + 1) * LANES], v, preferred_element_type=F32)
        acc_ref[...] = acc
        m_ref[...] = m_new

    for h in range(DIFF_KV_HEADS):
        q = qd_ref[0, h]
        ks = [dk_refs[i][0, :, h * LANES:(h + 1) * LANES].astype(BF16) for i in range(pps)]
        vs = [dv_refs[i][0, :, h * LANES:(h + 1) * LANES].astype(BF16) for i in range(pps)]
        s_list = [lax.dot_general(q, k, NT_DIMS, preferred_element_type=F32) * DIFF_SCALE for k in ks]
        merge(s_list, vs, md_s.at[h], ld_s.at[h], accd_s.at[h])

    ql = ql_ref[0]
    qr = qr_ref[0]
    cs = [ck_refs[i][0].astype(BF16) for i in range(pps)]
    s_list = [
        (lax.dot_general(ql, cs[i], NT_DIMS, preferred_element_type=F32)
         + lax.dot_general(qr, kr_refs[i][0].astype(BF16), NT_DIMS, preferred_element_type=F32)) * MLA_SCALE
        for i in range(pps)]
    merge(s_list, cs, mm_s, lm_s, accm_s)

    @pl.when(step == pl.num_programs(1) - 1)
    def _():
        for h in range(DIFF_KV_HEADS):
            o = accd_s[h] / ld_s[h]
            d = o[:DIFF_GROUP] - lam_ref[0] * o[DIFF_GROUP:]
            d = d * lax.rsqrt(jnp.mean(d * d, axis=-1, keepdims=True) + RMS_EPS) * g_ref[...] * (1.0 - lam_init)
            for g in range(DIFF_GROUP):
                c0 = (h * DIFF_GROUP + g) * DIFF_V_DIM
                od_ref[0, :, c0:c0 + DIFF_V_DIM] = d[g:g + 1].astype(BF16)
        o = accm_s[...] / lm_s[...]
        for h in range(MLA_HEADS):
            om_ref[0, :, h * MLA_KV_RANK:(h + 1) * MLA_KV_RANK] = o[h:h + 1].astype(BF16)


def _decode(page_table, lam, qblk, kd_new, vd_new, ql, qr, ckv_new, kr_new, subln_g,
            cache_dk, cache_dv, cache_ckv, cache_kr, pps, lam_init):
    bd, n_pages = page_table.shape
    page = cache_dk.shape[1]
    steps = n_pages // pps
    pt_flat = page_table.reshape(-1)

    def cache_spec(width, p):
        return pl.BlockSpec((1, page, width), lambda b, s, pt: (pt[b * n_pages + s * pps + p], 0, 0))

    per_seq = lambda shape: pl.BlockSpec((1,) + shape, lambda b, s, pt: (b,) + (0,) * len(shape))
    in_specs = [
        pl.BlockSpec(memory_space=pltpu.SMEM),
        per_seq((DIFF_KV_HEADS, 8, LANES)),
        per_seq((1, DIFF_K_COLS)), per_seq((1, DIFF_V_COLS)),
        per_seq((MLA_HEADS, MLA_KV_RANK)), per_seq((MLA_HEADS, MLA_ROPE_DIM)),
        per_seq((1, MLA_KV_RANK)), per_seq((1, MLA_ROPE_DIM)),
        pl.BlockSpec((1, DIFF_V_DIM), lambda b, s, pt: (0, 0)),
    ]
    caches = []
    for arr, width in ((cache_dk, DIFF_K_COLS), (cache_dv, DIFF_V_COLS), (cache_ckv, MLA_KV_RANK),
                       (cache_kr, MLA_ROPE_DIM)):
        for p in range(pps):
            in_specs.append(cache_spec(width, p))
            caches.append(arr)
    kern = functools.partial(_decode_kernel, pps=pps, lam_init=lam_init)
    od, om = pl.pallas_call(
        kern,
        grid_spec=pltpu.PrefetchScalarGridSpec(
            num_scalar_prefetch=1,
            grid=(bd, steps),
            in_specs=in_specs,
            out_specs=[per_seq((1, DIFF_KV_HEADS * DIFF_GROUP * DIFF_V_DIM)),
                       per_seq((1, MLA_HEADS * MLA_KV_RANK))],
            scratch_shapes=[
                pltpu.VMEM((DIFF_KV_HEADS, 8, 1), F32), pltpu.VMEM((DIFF_KV_HEADS, 8, 1), F32),
                pltpu.VMEM((DIFF_KV_HEADS, 8, DIFF_V_DIM), F32),
                pltpu.VMEM((8, 1), F32), pltpu.VMEM((8, 1), F32), pltpu.VMEM((8, MLA_KV_RANK), F32),
            ],
        ),
        out_shape=[jax.ShapeDtypeStruct((bd, 1, DIFF_KV_HEADS * DIFF_GROUP * DIFF_V_DIM), BF16),
                   jax.ShapeDtypeStruct((bd, 1, MLA_HEADS * MLA_KV_RANK), BF16)],
        compiler_params=_cparams(("parallel", "arbitrary")),
    )(pt_flat, lam, qblk, kd_new, vd_new, ql, qr, ckv_new, kr_new, subln_g, *caches)
    return od.reshape(bd, -1), om.reshape(bd, -1)


def _layer_norm(y, g, b):
    yc = y - jnp.mean(y, axis=-1, keepdims=True)
    var = jnp.mean(yc * yc, axis=-1, keepdims=True)
    return yc * lax.rsqrt(var + LN_EPS) * g + b


def _out_ln1_kernel(x_ref, od_ref, oml_ref, wv_ref, wo_ref, g_ref, b_ref, x1_ref, x1b_ref, mix_s, *, alpha):
    half = od_ref.shape[1]
    mix_s[:, :half] = od_ref[...]
    for h in range(MLA_HEADS):
        om = jnp.dot(oml_ref[:, h * MLA_KV_RANK:(h + 1) * MLA_KV_RANK], wv_ref[h], preferred_element_type=F32)
        mix_s[:, half + h * MLA_V_DIM:half + (h + 1) * MLA_V_DIM] = om.astype(BF16)
    y = alpha * x_ref[...] + jnp.dot(mix_s[...], wo_ref[...], preferred_element_type=F32)
    x1 = _layer_norm(y, g_ref[...], b_ref[...])
    x1_ref[...] = x1
    x1b_ref[...] = x1.astype(BF16)


def _out_ln1(x2d, od, oml, wv, wo, g, b, tm, alpha):
    t = x2d.shape[0]
    row = lambda w: pl.BlockSpec((tm, w), lambda i: (i, 0))
    full = lambda a: pl.BlockSpec(a.shape, lambda i: (0,) * a.ndim)
    return pl.pallas_call(
        functools.partial(_out_ln1_kernel, alpha=alpha),
        grid=(t // tm,),
        in_specs=[row(D_MODEL), row(od.shape[1]), row(oml.shape[1]), full(wv), full(wo), full(g), full(b)],
        out_specs=[row(D_MODEL), row(D_MODEL)],
        out_shape=[jax.ShapeDtypeStruct((t, D_MODEL), F32), jax.ShapeDtypeStruct((t, D_MODEL), BF16)],
        scratch_shapes=[pltpu.VMEM((tm, D_MODEL), BF16)],
        compiler_params=_cparams(("parallel",)),
    )(x2d, od, oml, wv, wo, g, b)


def _peer_query_kernel(x1b_ref, wq_ref, qg_ref, sk_ref, s2_ref, ca_ref, ea_ref, eb_ref,
                       t1_s, t2_s, cand_s):
    tm = x1b_ref.shape[0]
    q = jnp.dot(x1b_ref[...], wq_ref[...], preferred_element_type=F32)
    half = PEER_KEY_DIM // 2
    neg = -jnp.inf

    def top_values(s, t_s):
        work = s
        for i in range(PEER_TOPK):
            m = jnp.max(work, axis=0, keepdims=True)
            t_s[i:i + 1, :] = m
            work = jnp.where(work == m, neg, work)

    for h in range(PEER_HEADS):
        qh = q[:, h * PEER_KEY_DIM:(h + 1) * PEER_KEY_DIM]
        qn = (qh * lax.rsqrt(jnp.mean(qh * qh, axis=-1, keepdims=True) + RMS_EPS) * qg_ref[h]).astype(BF16)
        s1 = lax.dot_general(sk_ref[h, 0], qn[:, :half], NT_DIMS, preferred_element_type=F32)
        s2 = lax.dot_general(sk_ref[h, 1], qn[:, half:], NT_DIMS, preferred_element_type=F32)
        top_values(s1, t1_s)
        top_values(s2, t2_s)
        t2 = t2_s[...]
        for i in range(PEER_TOPK):
            cand_s[i * PEER_TOPK:(i + 1) * PEER_TOPK, :] = t1_s[i:i + 1, :] + t2
        work = cand_s[...]
        thr = None
        for i in range(PEER_TOPK):
            thr = jnp.max(work, axis=0, keepdims=True)
            work = jnp.where(work == thr, neg, work)
        top = t1_s[0:1, :] + t2_s[0:1, :]
        cand = cand_s[...]
        z = jnp.sum(jnp.where(cand >= thr, jnp.exp(cand - top), 0.0), axis=0, keepdims=True)
        ca = jnp.full((PEER_N_KEYS, tm), jnp.inf, F32)
        for i in range(PEER_TOPK):
            t1i = t1_s[i:i + 1, :]
            ci = jnp.min(jnp.where(t1i + t2 >= thr, t2, jnp.inf), axis=0, keepdims=True)
            ca = jnp.where(s1 == t1i, ci, ca)
        s2_ref[h] = s2
        ca_ref[h] = ca
        ea_ref[h] = jnp.exp(s1 - t1_s[0:1, :]) / z
        eb_ref[h] = jnp.exp(s2 - t2_s[0:1, :])


def _peer_query(x1b, wq, qg, subkeys, tm):
    t = x1b.shape[0]
    full = lambda a: pl.BlockSpec(a.shape, lambda i: (0,) * a.ndim)
    kt = pl.BlockSpec((PEER_HEADS, PEER_N_KEYS, tm), lambda i: (0, 0, i))
    shp = jax.ShapeDtypeStruct((PEER_HEADS, PEER_N_KEYS, t), F32)
    return pl.pallas_call(
        _peer_query_kernel,
        grid=(t // tm,),
        in_specs=[pl.BlockSpec((tm, D_MODEL), lambda i: (i, 0)), full(wq), full(qg), full(subkeys)],
        out_specs=[kt, kt, kt, kt],
        out_shape=[shp, shp, shp, shp],
        scratch_shapes=[pltpu.VMEM((PEER_TOPK, tm), F32), pltpu.VMEM((PEER_TOPK, tm), F32),
                        pltpu.VMEM((PEER_TOPK * PEER_TOPK, tm), F32)],
        compiler_params=_cparams(("parallel",)),
    )(x1b, wq, qg, subkeys)


def _gelu(x):
    return 0.5 * x * (1.0 + lax.erf(x * (2.0 ** -0.5)))


def _peer_dense_kernel(x1b_ref, u_ref, vt_ref, s2_ref, eb_ref, ca_ref, ea_ref, f_ref, w_s, acc_s):
    j = pl.program_id(1)
    te = u_ref.shape[0]
    tm = x1b_ref.shape[0]

    @pl.when(j == 0)
    def _():
        acc_s[...] = jnp.zeros(acc_s.shape, F32)

    ht = lax.dot_general(u_ref[...], x1b_ref[...], NT_DIMS, preferred_element_type=F32)
    w_s[...] = _gelu(ht)

    def per_key(k, carry):
        ca = [ca_ref[h, pl.ds(k, 1), :] for h in range(PEER_HEADS)]
        ea = [ea_ref[h, pl.ds(k, 1), :] for h in range(PEER_HEADS)]
        base = pl.multiple_of(k * PEER_N_KEYS, PEER_N_KEYS)
        for bc in range(PEER_N_KEYS // 8):
            g = jnp.zeros((8, tm), F32)
            for h in range(PEER_HEADS):
                sel = s2_ref[h, bc * 8:(bc + 1) * 8, :] >= ca[h]
                g = g + jnp.where(sel, eb_ref[h, bc * 8:(bc + 1) * 8, :], 0.0) * ea[h]
            rows = pl.ds(base + bc * 8, 8)
            w_s[rows, :] = w_s[rows, :] * g
        return carry

    lax.fori_loop(0, te // PEER_N_KEYS, per_key, 0)
    acc_s[...] += jnp.dot(vt_ref[...], w_s[...].astype(BF16), preferred_element_type=F32)

    @pl.when(j == pl.num_programs(1) - 1)
    def _():
        f_ref[...] = acc_s[...].T


def _peer_dense(x1b, u_b, vt_b, s2t, cat, eat, ebt, tm, te):
    t = x1b.shape[0]
    n_exp = u_b.shape[0]
    ka = te // PEER_N_KEYS
    tok = pl.BlockSpec((PEER_HEADS, PEER_N_KEYS, tm), lambda i, j: (0, 0, i))
    key = pl.BlockSpec((PEER_HEADS, ka, tm), lambda i, j: (0, j, i))
    return pl.pallas_call(
        _peer_dense_kernel,
        grid=(t // tm, n_exp // te),
        in_specs=[
            pl.BlockSpec((tm, D_MODEL), lambda i, j: (i, 0)),
            pl.BlockSpec((te, D_MODEL), lambda i, j: (j, 0)),
            pl.BlockSpec((D_MODEL, te), lambda i, j: (0, j)),
            tok, tok, key, key,
        ],
        out_specs=pl.BlockSpec((tm, D_MODEL), lambda i, j: (i, 0)),
        out_shape=jax.ShapeDtypeStruct((t, D_MODEL), F32),
        scratch_shapes=[pltpu.VMEM((te, tm), F32), pltpu.VMEM((D_MODEL, tm), F32)],
        compiler_params=_cparams(("parallel", "arbitrary")),
    )(x1b, u_b, vt_b, s2t, ebt, cat, eat)


def _ln2_kernel(x1_ref, f_ref, g_ref, b_ref, o_ref, *, alpha):
    o_ref[...] = _layer_norm(alpha * x1_ref[...] + f_ref[...], g_ref[...], b_ref[...])


def _ln2(x1, f, g, b, tm, alpha):
    t = x1.shape[0]
    row = pl.BlockSpec((tm, D_MODEL), lambda i: (i, 0))
    vec = pl.BlockSpec((1, D_MODEL), lambda i: (0, 0))
    return pl.pallas_call(
        functools.partial(_ln2_kernel, alpha=alpha),
        grid=(t // tm,),
        in_specs=[row, row, vec, vec],
        out_specs=row,
        out_shape=jax.ShapeDtypeStruct((t, D_MODEL), F32),
        compiler_params=_cparams(("parallel",)),
    )(x1, f, g, b)


def _rope_tables(pos):
    half = DIFF_HEAD_DIM // 2
    inv_freq = ROPE_THETA ** (-jnp.arange(half, dtype=F32) / half)
    ang = pos.astype(F32)[:, None] * inv_freq[None, :]
    cos = jnp.cos(ang)
    sin = jnp.sin(ang)
    return jnp.tile(cos, (1, 4)), jnp.tile(jnp.concatenate([-sin, sin], axis=-1), (1, 2))


def _pick_tile(t, pref):
    tm = min(t, pref)
    assert t % tm == 0, (t, tm)
    return tm


def kernel(x_prompt, x_sample, cache_diff_k, cache_diff_v, cache_mla_ckv, cache_mla_krope, page_table, w_in, diff_lambda_q1, diff_lambda_k1, diff_lambda_q2, diff_lambda_k2, diff_subln_g, mla_q_norm_g, mla_kv_norm_g, w_uq, w_ukv, w_out, ln1_g, ln1_b, ln2_g, ln2_b, peer_wq, peer_q_norm_g, peer_subkeys, peer_u, peer_v):
    depth = w_in.shape[0]
    assert depth == 1, "single-layer trunk"
    layer = 0
    batch, seq, d = x_prompt.shape
    bd, dec_seq, _ = x_sample.shape
    assert d == D_MODEL and dec_seq == 1
    n_pool, page = cache_diff_k.shape[1], cache_diff_k.shape[2]
    past_len = page_table.shape[1] * page
    alpha = (2 * depth) ** 0.25
    lam_init = 0.8 - 0.6 * math.exp(-0.3 * layer)

    lam = (jnp.exp(jnp.sum(diff_lambda_q1[layer] * diff_lambda_k1[layer]))
           - jnp.exp(jnp.sum(diff_lambda_q2[layer] * diff_lambda_k2[layer])) + lam_init).reshape(1).astype(F32)

    w_in_l = w_in[layer]
    w_in_p = jnp.concatenate([w_in_l, w_in_l[:, OFF_KR:OFF_KR + MLA_ROPE_DIM]], axis=1).astype(BF16)
    w_uq_l = w_uq[layer]
    w_uq_p = jnp.concatenate([
        w_uq_l[:, :, :MLA_NOPE_DIM].reshape(MLA_Q_RANK, -1),
        w_uq_l[:, :, MLA_NOPE_DIM:].reshape(MLA_Q_RANK, -1)], axis=1).astype(BF16)
    w_ukv_l = w_ukv[layer]
    wn = w_ukv_l[:, :, :MLA_NOPE_DIM].transpose(1, 2, 0).astype(BF16)
    wv = w_ukv_l[:, :, MLA_NOPE_DIM:].transpose(1, 0, 2).astype(BF16)
    wo = w_out[layer].astype(BF16)
    wq = peer_wq[layer].astype(BF16)
    subkeys = peer_subkeys[layer].astype(BF16)
    u_b = peer_u[layer].astype(BF16)
    vt_b = peer_v[layer].T.astype(BF16)
    qg = mla_q_norm_g[layer].reshape(1, -1)
    kvg = mla_kv_norm_g[layer].reshape(1, -1)
    subln = diff_subln_g[layer].reshape(1, -1)
    pqg = peer_q_norm_g[layer].reshape(PEER_HEADS, 1, PEER_KEY_DIM)
    g1, b1 = ln1_g[layer].reshape(1, -1), ln1_b[layer].reshape(1, -1)
    g2, b2 = ln2_g[layer].reshape(1, -1), ln2_b[layer].reshape(1, -1)

    def project(x2d, pos, tm):
        cos, sin = _rope_tables(pos)
        return _proj(x2d, cos, sin, w_in_p, qg, w_uq_p, kvg, wn, tm)

    def block_out(x2d, od, oml, tm, tm_peer, te):
        x1, x1b = _out_ln1(x2d, od, oml, wv, wo, g1, b1, tm, alpha)
        s2t, cat, eat, ebt = _peer_query(x1b, wq, pqg, subkeys, tm)
        f = _peer_dense(x1b, u_b, vt_b, s2t, cat, eat, ebt, tm_peer, te)
        return _ln2(x1, f, g2, b2, tm, alpha)

    tp = batch * seq
    xp = x_prompt.reshape(tp, d)
    tm_p = _pick_tile(tp, 256)
    pos_p = jnp.tile(jnp.arange(seq, dtype=jnp.int32), batch)
    qd, kd, vd, ckv, kr, kvd, kcat, qlat, qrope = project(xp, pos_p, tm_p)
    tq = _pick_tile(seq, 256)
    od = _diff_flash(lam, qd, kvd, subln, batch, seq, tq, lam_init)
    oml = _mla_flash(qlat, qrope, kcat, batch, seq, tq)
    y_p = block_out(xp, od, oml, tm_p, _pick_tile(tp, 512), 1024).reshape(batch, seq, d)
    p_dk = kd.reshape(1, batch, seq, DIFF_KV_HEADS, 2, DIFF_HEAD_DIM)
    p_dv = vd.reshape(1, batch, seq, DIFF_KV_HEADS, DIFF_V_DIM)
    p_ckv = ckv.reshape(1, batch, seq, MLA_KV_RANK)
    p_kr = kr.reshape(1, batch, seq, MLA_ROPE_DIM)

    xs = x_sample.reshape(bd, d)
    tm_s = _pick_tile(bd, 128)
    pos_s = jnp.full((bd,), past_len, jnp.int32)
    qd, kd, vd, ckv, kr, kvd, kcat, qlat, qrope = project(xs, pos_s, tm_s)
    q5 = qd.reshape(bd, DIFF_KV_HEADS, DIFF_GROUP, 2, DIFF_HEAD_DIM)
    zeros = jnp.zeros_like(q5[:, :, :, 0])
    qblk = jnp.concatenate([
        jnp.concatenate([q5[:, :, :, 0], zeros], axis=-1),
        jnp.concatenate([zeros, q5[:, :, :, 1]], axis=-1)], axis=2)
    od_s, oml_s = _decode(
        page_table, lam, qblk, kd.reshape(bd, 1, -1), vd.reshape(bd, 1, -1),
        qlat.reshape(bd, MLA_HEADS, MLA_KV_RANK), qrope.reshape(bd, MLA_HEADS, MLA_ROPE_DIM),
        ckv.reshape(bd, 1, -1), kr.reshape(bd, 1, -1), subln,
        cache_diff_k.reshape(depth * n_pool, page, DIFF_K_COLS),
        cache_diff_v.reshape(depth * n_pool, page, DIFF_V_COLS),
        cache_mla_ckv.reshape(depth * n_pool, page, MLA_KV_RANK),
        cache_mla_krope.reshape(depth * n_pool, page, MLA_ROPE_DIM),
        pps=_pick_tile(page_table.shape[1], 8), lam_init=lam_init)
    y_s = block_out(xs, od_s, oml_s, tm_s, tm_s, 1024).reshape(bd, 1, d)
    s_dk = kd.reshape(1, bd, 1, DIFF_KV_HEADS, 2, DIFF_HEAD_DIM)
    s_dv = vd.reshape(1, bd, 1, DIFF_KV_HEADS, DIFF_V_DIM)
    s_ckv = ckv.reshape(1, bd, 1, MLA_KV_RANK)
    s_kr = kr.reshape(1, bd, 1, MLA_ROPE_DIM)

    return (y_p, y_s, p_dk, p_dv, p_ckv, p_kr, s_dk, s_dv, s_ckv, s_kr)
```

```python
import functools
import math

import jax
import jax.numpy as jnp
from jax import lax
from jax.experimental import pallas as pl
from jax.experimental.pallas import tpu as pltpu

BF16 = jnp.bfloat16
F32 = jnp.float32

D_MODEL = 2048
DIFF_HEAD_DIM = 64
DIFF_V_DIM = 128
DIFF_KV_HEADS = 2
DIFF_GROUP = 4
DIFF_SCALE = DIFF_HEAD_DIM ** -0.5
MLA_NOPE_DIM = 128
MLA_ROPE_DIM = 64
MLA_V_DIM = 128
MLA_HEADS = 8
MLA_Q_RANK = 512
MLA_KV_RANK = 256
MLA_SCALE = (MLA_NOPE_DIM + MLA_ROPE_DIM) ** -0.5
PEER_HEADS = 8
PEER_KEY_DIM = 256
PEER_N_KEYS = 128
PEER_TOPK = 16
PEER_CAND_ROWS = -(-sum(PEER_TOPK // (i + 1) for i in range(PEER_TOPK)) // 8) * 8
ROPE_THETA = 10000.0
LN_EPS = 1e-5
RMS_EPS = 1e-6

DIFF_Q_COLS = 1024
DIFF_K_COLS = 256
DIFF_V_COLS = 256
OFF_K = DIFF_Q_COLS
OFF_V = OFF_K + DIFF_K_COLS
OFF_CQ = OFF_V + DIFF_V_COLS
OFF_CKV = OFF_CQ + MLA_Q_RANK
OFF_KR = OFF_CKV + MLA_KV_RANK
D_IN_PAD = OFF_KR + 128

LANES = 128
VMEM_LIMIT = 56 * 1024 * 1024

NT_DIMS = (((1,), (1,)), ((), ()))


def _cparams(sem):
    return pltpu.CompilerParams(dimension_semantics=sem, vmem_limit_bytes=VMEM_LIMIT)


def _rope_chunk(c, cos, sin_signed, first_half):
    sw = jnp.where(first_half, pltpu.roll(c, 96, 1), pltpu.roll(c, 32, 1))
    return c * cos + sw * sin_signed


def _proj_kernel(x_ref, cos_ref, sin_ref, win_ref, qg_ref, wuq_ref, kvg_ref, wn_ref,
                 qd_ref, kd_ref, vd_ref, ckv_ref, kr_ref, kdb_ref, kcat_ref, qlat_ref, qrope_ref,
                 vdt_ref, ckvt_ref):
    tm = x_ref.shape[0]
    z = jnp.dot(x_ref[...].astype(BF16), win_ref[...], preferred_element_type=F32)
    cos = cos_ref[...]
    sin = sin_ref[...]
    lane = lax.broadcasted_iota(jnp.int32, (tm, LANES), 1)
    first = (lane % 64) < 32

    def rope(c):
        return _rope_chunk(c, cos, sin, first)

    for j in range(DIFF_Q_COLS // LANES):
        qd_ref[:, j * LANES:(j + 1) * LANES] = rope(z[:, j * LANES:(j + 1) * LANES]).astype(BF16)
    for j in range(DIFF_K_COLS // LANES):
        k = rope(z[:, OFF_K + j * LANES:OFF_K + (j + 1) * LANES])
        kd_ref[:, j * LANES:(j + 1) * LANES] = k
        kdb_ref[:, j * LANES:(j + 1) * LANES] = k.astype(BF16)
    v = z[:, OFF_V:OFF_V + DIFF_V_COLS]
    vd_ref[...] = v
    vdt_ref[...] = v.T.astype(BF16)

    ckv = z[:, OFF_CKV:OFF_CKV + MLA_KV_RANK]
    ckv = ckv * lax.rsqrt(jnp.mean(ckv * ckv, axis=-1, keepdims=True) + RMS_EPS) * kvg_ref[...]
    ckv_ref[...] = ckv
    kcat_ref[:, :MLA_KV_RANK] = ckv.astype(BF16)
    ckvt_ref[...] = ckv.T.astype(BF16)
    kr2 = rope(z[:, OFF_KR:OFF_KR + LANES])
    kr_ref[...] = kr2[:, :MLA_ROPE_DIM]
    kcat_ref[:, MLA_KV_RANK:] = kr2.astype(BF16)

    cq = z[:, OFF_CQ:OFF_CQ + MLA_Q_RANK]
    cq = cq * lax.rsqrt(jnp.mean(cq * cq, axis=-1, keepdims=True) + RMS_EPS) * qg_ref[...]
    q = jnp.dot(cq.astype(BF16), wuq_ref[...], preferred_element_type=F32)
    nope_cols = MLA_HEADS * MLA_NOPE_DIM
    for j in range(MLA_HEADS * MLA_ROPE_DIM // LANES):
        qrope_ref[:, j * LANES:(j + 1) * LANES] = rope(
            q[:, nope_cols + j * LANES:nope_cols + (j + 1) * LANES]).astype(BF16)
    for h in range(MLA_HEADS):
        qn = q[:, h * MLA_NOPE_DIM:(h + 1) * MLA_NOPE_DIM].astype(BF16)
        qlat_ref[:, h * MLA_KV_RANK:(h + 1) * MLA_KV_RANK] = jnp.dot(
            qn, wn_ref[h], preferred_element_type=F32).astype(BF16)


def _proj(x2d, cos, sin, w_in_p, q_norm_g, w_uq_p, kv_norm_g, wn, tm):
    t = x2d.shape[0]
    row = lambda w: pl.BlockSpec((tm, w), lambda i: (i, 0))
    full = lambda a: pl.BlockSpec(a.shape, lambda i: (0,) * a.ndim)
    outs = [
        (DIFF_Q_COLS, BF16), (DIFF_K_COLS, F32), (DIFF_V_COLS, F32), (MLA_KV_RANK, F32),
        (MLA_ROPE_DIM, F32), (DIFF_K_COLS, BF16), (MLA_KV_RANK + LANES, BF16),
        (MLA_HEADS * MLA_KV_RANK, BF16), (MLA_HEADS * MLA_ROPE_DIM, BF16),
    ]
    col = lambda h: pl.BlockSpec((h, tm), lambda i: (0, i))
    return pl.pallas_call(
        _proj_kernel,
        grid=(t // tm,),
        in_specs=[row(D_MODEL), row(LANES), row(LANES), full(w_in_p), full(q_norm_g), full(w_uq_p),
                  full(kv_norm_g), full(wn)],
        out_specs=[row(w) for w, _ in outs] + [col(DIFF_V_COLS), col(MLA_KV_RANK)],
        out_shape=[jax.ShapeDtypeStruct((t, w), dt) for w, dt in outs]
        + [jax.ShapeDtypeStruct((DIFF_V_COLS, t), BF16), jax.ShapeDtypeStruct((MLA_KV_RANK, t), BF16)],
        compiler_params=_cparams(("parallel",)),
        name="proj",
    )(x2d, cos, sin, w_in_p, q_norm_g, w_uq_p, kv_norm_g, wn)


def _online_update(st, vt, m_s, l_s, acc_s):
    m_prev = m_s[...]
    m_new = jnp.maximum(m_prev, jnp.max(st, axis=0, keepdims=True))
    corr = jnp.exp(m_prev - m_new)
    p = jnp.exp(st - m_new)
    l_s[...] = l_s[...] * corr + jnp.sum(p, axis=0, keepdims=True)
    acc_s[...] = acc_s[...] * corr + jnp.dot(vt, p.astype(BF16), preferred_element_type=F32)
    m_s[...] = m_new


def _causal_mask(st, tq):
    keys = lax.broadcasted_iota(jnp.int32, st.shape, 0)
    queries = lax.broadcasted_iota(jnp.int32, st.shape, 1) & (tq - 1)
    return jnp.where(keys <= queries, st, -jnp.inf)


def _diff_flash_kernel(lam_ref, q_ref, k_ref, v_ref, g_ref, o_ref, qs, m_s, l_s, acc_s, *, tq, lam_init):
    qi = pl.program_id(2)
    kj = pl.program_id(3)
    ngc = 2 * DIFF_GROUP

    @pl.when(kj == 0)
    def _():
        lane = lax.broadcasted_iota(jnp.int32, (tq, LANES), 1)
        zero = jnp.zeros((tq, LANES), BF16)
        for g in range(DIFF_GROUP):
            chunk = q_ref[:, g * LANES:(g + 1) * LANES]
            qs[g * tq:(g + 1) * tq, :] = jnp.where(lane < 64, chunk, zero)
            qs[(DIFF_GROUP + g) * tq:(DIFF_GROUP + g + 1) * tq, :] = jnp.where(lane >= 64, chunk, zero)
        m_s[...] = jnp.full(m_s.shape, -jnp.inf, F32)
        l_s[...] = jnp.zeros(l_s.shape, F32)
        acc_s[...] = jnp.zeros(acc_s.shape, F32)

    def scores():
        return lax.dot_general(k_ref[...], qs[...], NT_DIMS, preferred_element_type=F32) * DIFF_SCALE

    @pl.when(kj < qi)
    def _():
        _online_update(scores(), v_ref[...], m_s, l_s, acc_s)

    @pl.when(kj == qi)
    def _():
        _online_update(_causal_mask(scores(), tq), v_ref[...], m_s, l_s, acc_s)

    @pl.when(kj == pl.num_programs(3) - 1)
    def _():
        o = acc_s[...] / l_s[...]
        d = o[:, :DIFF_GROUP * tq] - lam_ref[0] * o[:, DIFF_GROUP * tq:]
        d = d * lax.rsqrt(jnp.mean(d * d, axis=0, keepdims=True) + RMS_EPS) * g_ref[...] * (1.0 - lam_init)
        for g in range(DIFF_GROUP):
            o_ref[:, g * DIFF_V_DIM:(g + 1) * DIFF_V_DIM] = d[:, g * tq:(g + 1) * tq].T.astype(BF16)


def _diff_flash(lam, qd, kdb, vdt, subln_col, batch, seq, tq, lam_init):
    nq = seq // tq
    ngc = 2 * DIFF_GROUP
    kern = functools.partial(_diff_flash_kernel, tq=tq, lam_init=lam_init)
    gw = DIFF_GROUP * 2 * DIFF_HEAD_DIM
    return pl.pallas_call(
        kern,
        grid=(batch, DIFF_KV_HEADS, nq, nq),
        in_specs=[
            pl.BlockSpec(memory_space=pltpu.SMEM),
            pl.BlockSpec((tq, gw), lambda b, h, i, j: (b * nq + i, h)),
            pl.BlockSpec((tq, LANES), lambda b, h, i, j: (b * nq + jnp.minimum(i, j), h)),
            pl.BlockSpec((DIFF_V_DIM, tq), lambda b, h, i, j: (h, b * nq + jnp.minimum(i, j))),
            pl.BlockSpec((DIFF_V_DIM, 1), lambda b, h, i, j: (0, 0)),
        ],
        out_specs=pl.BlockSpec((tq, DIFF_GROUP * DIFF_V_DIM), lambda b, h, i, j: (b * nq + i, h)),
        out_shape=jax.ShapeDtypeStruct((batch * seq, DIFF_KV_HEADS * DIFF_GROUP * DIFF_V_DIM), BF16),
        scratch_shapes=[
            pltpu.VMEM((ngc * tq, LANES), BF16),
            pltpu.VMEM((1, ngc * tq), F32),
            pltpu.VMEM((1, ngc * tq), F32),
            pltpu.VMEM((DIFF_V_DIM, ngc * tq), F32),
        ],
        compiler_params=_cparams(("parallel", "parallel", "parallel", "arbitrary")),
        name="diff_flash",
    )(lam, qd, kdb, vdt, subln_col)


def _mla_flash_kernel(ql_ref, qr_ref, k_ref, vt_ref, o_ref, qs, m_s, l_s, acc_s, *, tq):
    qi = pl.program_id(1)
    kj = pl.program_id(2)

    @pl.when(kj == 0)
    def _():
        lane = lax.broadcasted_iota(jnp.int32, (tq, LANES), 1)
        zero = jnp.zeros((tq, LANES), BF16)
        for h in range(MLA_HEADS):
            qs[h * tq:(h + 1) * tq, :MLA_KV_RANK] = ql_ref[:, h * MLA_KV_RANK:(h + 1) * MLA_KV_RANK]
            chunk = qr_ref[:, (h // 2) * LANES:(h // 2 + 1) * LANES]
            keep = (lane < 64) if h % 2 == 0 else (lane >= 64)
            qs[h * tq:(h + 1) * tq, MLA_KV_RANK:] = jnp.where(keep, chunk, zero)
        m_s[...] = jnp.full(m_s.shape, -jnp.inf, F32)
        l_s[...] = jnp.zeros(l_s.shape, F32)
        acc_s[...] = jnp.zeros(acc_s.shape, F32)

    def scores():
        return lax.dot_general(k_ref[...], qs[...], NT_DIMS, preferred_element_type=F32) * MLA_SCALE

    @pl.when(kj < qi)
    def _():
        _online_update(scores(), vt_ref[...], m_s, l_s, acc_s)

    @pl.when(kj == qi)
    def _():
        _online_update(_causal_mask(scores(), tq), vt_ref[...], m_s, l_s, acc_s)

    @pl.when(kj == pl.num_programs(2) - 1)
    def _():
        o = acc_s[...] / l_s[...]
        for h in range(MLA_HEADS):
            o_ref[:, h * MLA_KV_RANK:(h + 1) * MLA_KV_RANK] = o[:, h * tq:(h + 1) * tq].T.astype(BF16)


def _mla_flash(qlat, qrope, kcat, ckvt, batch, seq, tq):
    nq = seq // tq
    kern = functools.partial(_mla_flash_kernel, tq=tq)
    kw = MLA_KV_RANK + LANES
    return pl.pallas_call(
        kern,
        grid=(batch, nq, nq),
        in_specs=[
            pl.BlockSpec((tq, MLA_HEADS * MLA_KV_RANK), lambda b, i, j: (b * nq + i, 0)),
            pl.BlockSpec((tq, MLA_HEADS * MLA_ROPE_DIM), lambda b, i, j: (b * nq + i, 0)),
            pl.BlockSpec((tq, kw), lambda b, i, j: (b * nq + jnp.minimum(i, j), 0)),
            pl.BlockSpec((MLA_KV_RANK, tq), lambda b, i, j: (0, b * nq + jnp.minimum(i, j))),
        ],
        out_specs=pl.BlockSpec((tq, MLA_HEADS * MLA_KV_RANK), lambda b, i, j: (b * nq + i, 0)),
        out_shape=jax.ShapeDtypeStruct((batch * seq, MLA_HEADS * MLA_KV_RANK), BF16),
        scratch_shapes=[
            pltpu.VMEM((MLA_HEADS * tq, kw), BF16),
            pltpu.VMEM((1, MLA_HEADS * tq), F32),
            pltpu.VMEM((1, MLA_HEADS * tq), F32),
            pltpu.VMEM((MLA_KV_RANK, MLA_HEADS * tq), F32),
        ],
        compiler_params=_cparams(("parallel", "parallel", "arbitrary")),
        name="mla_flash",
    )(qlat, qrope, kcat, ckvt)


def _decode_kernel(pt_ref, lam_ref, qd_ref, kdn_ref, vdn_ref, ql_ref, qr_ref, ckvn_ref, krn_ref, g_ref,
                   dk_hbm, dv_hbm, ck_hbm, kr_hbm, od_ref, om_ref,
                   dk_buf, dv_buf, ck_buf, kr_buf, sems, md_s, ld_s, accd_s, mm_s, lm_s, accm_s,
                   *, pps, n_pages, lam_init):
    b = pl.program_id(0)
    n_steps = n_pages // pps
    page = ck_buf.shape[2]
    hbm_bufs =((dk_hbm, dk_buf), (dv_hbm, dv_buf), (ck_hbm, ck_buf), (kr_hbm, kr_buf))

    def page_copy(a, page_id, slot, p):
        hbm, buf = hbm_bufs[a]
        return pltpu.make_async_copy(hbm.at[page_id], buf.at[slot, p], sems.at[a, slot])

    def start_fetch(seq, step, slot):
        for p in range(pps):
            page_id = pt_ref[seq * n_pages + step * pps + p]
            for a in range(len(hbm_bufs)):
                page_copy(a, page_id, slot, p).start()

    def wait_fetch(slot):
        for p in range(pps):
            for a in range(len(hbm_bufs)):
                page_copy(a, 0, slot, p).wait()

    @pl.when(b == 0)
    def _():
        start_fetch(0, 0, 0)

    def init():
        for h in range(DIFF_KV_HEADS):
            q = qd_ref[0, h].astype(F32)
            kn = kdn_ref[0, :, h * LANES:(h + 1) * LANES]
            md_s[h] = jnp.sum(q * kn.astype(BF16).astype(F32), axis=-1, keepdims=True) * DIFF_SCALE
            ld_s[h] = jnp.ones((8, 1), F32)
            accd_s[h] = jnp.broadcast_to(vdn_ref[0, :, h * LANES:(h + 1) * LANES].astype(BF16).astype(F32),
                                         (8, DIFF_V_DIM))
        ql = ql_ref[0].astype(F32)
        qr = qr_ref[0].astype(F32)
        cn = ckvn_ref[0].astype(BF16).astype(F32)
        rn = krn_ref[0].astype(BF16).astype(F32)
        mm_s[...] = (jnp.sum(ql * cn, axis=-1, keepdims=True)
                     + jnp.sum(qr * rn, axis=-1, keepdims=True)) * MLA_SCALE
        lm_s[...] = jnp.ones((8, 1), F32)
        accm_s[...] = jnp.broadcast_to(cn, (8, MLA_KV_RANK))

    def merge(s_list, v_list, m_ref, l_ref, acc_ref):
        s = jnp.concatenate(s_list, axis=-1)
        m_prev = m_ref[...]
        m_new = jnp.maximum(m_prev, jnp.max(s, axis=-1, keepdims=True))
        corr = jnp.exp(m_prev - m_new)
        p = jnp.exp(s - m_new)
        l_ref[...] = l_ref[...] * corr + jnp.sum(p, axis=-1, keepdims=True)
        pb = p.astype(BF16)
        acc = acc_ref[...] * corr
        for i, v in enumerate(v_list):
            acc = acc + jnp.dot(pb[:, i * LANES:(i + 1) * LANES], v, preferred_element_type=F32)
        acc_ref[...] = acc
        m_ref[...] = m_new

    def compute(slot):
        for h in range(DIFF_KV_HEADS):
            q = qd_ref[0, h]
            s_list = [
                jnp.dot(q, dk_buf[slot, p, h * LANES:(h + 1) * LANES, :].astype(BF16),
                        preferred_element_type=F32) * DIFF_SCALE
                for p in range(pps)]
            vs = [dv_buf[slot, p, pl.ds(h, page, stride=DIFF_KV_HEADS), :].astype(BF16) for p in range(pps)]
            merge(s_list, vs, md_s.at[h], ld_s.at[h], accd_s.at[h])
        ql = ql_ref[0]
        qr = qr_ref[0]
        cs = [ck_buf[slot, p].astype(BF16) for p in range(pps)]
        s_list = [
            (lax.dot_general(ql, cs[p], NT_DIMS, preferred_element_type=F32)
             + jnp.dot(qr, kr_buf[slot, p].astype(BF16), preferred_element_type=F32)) * MLA_SCALE
            for p in range(pps)]
        merge(s_list, cs, mm_s, lm_s, accm_s)

    def body(step, carry):
        slot = (b * n_steps + step) & 1

        @pl.when(step + 1 < n_steps)
        def _():
            start_fetch(b, step + 1, 1 - slot)

        @pl.when(jnp.logical_and(step + 1 == n_steps, b + 1 < pl.num_programs(0)))
        def _():
            start_fetch(b + 1, 0, 1 - slot)

        wait_fetch(slot)
        compute(slot)
        return carry

    init()
    lax.fori_loop(0, n_steps, body, 0)

    def finish():
        for h in range(DIFF_KV_HEADS):
            o = accd_s[h] / ld_s[h]
            d = o[:DIFF_GROUP] - lam_ref[0] * o[DIFF_GROUP:]
            d = d * lax.rsqrt(jnp.mean(d * d, axis=-1, keepdims=True) + RMS_EPS) * g_ref[...] * (1.0 - lam_init)
            for g in range(DIFF_GROUP):
                c0 = (h * DIFF_GROUP + g) * DIFF_V_DIM
                od_ref[0, :, c0:c0 + DIFF_V_DIM] = d[g:g + 1].astype(BF16)
        o = accm_s[...] / lm_s[...]
        for h in range(MLA_HEADS):
            om_ref[0, :, h * MLA_KV_RANK:(h + 1) * MLA_KV_RANK] = o[h:h + 1].astype(BF16)

    finish()


def _decode(page_table, lam, qblk, kd_new, vd_new, ql, qr, ckv_new, kr_new, subln_g,
            cache_dk, cache_dv, cache_ckv, cache_kr, pps, lam_init):
    bd, n_pages = page_table.shape
    pt_flat = page_table.reshape(-1)
    per_seq = lambda shape: pl.BlockSpec((1,) + shape, lambda b, pt: (b,) + (0,) * len(shape))
    hbm = pl.BlockSpec(memory_space=pl.ANY)
    in_specs = [
        pl.BlockSpec(memory_space=pltpu.SMEM),
        per_seq((DIFF_KV_HEADS, 8, LANES)),
        per_seq((1, DIFF_K_COLS)), per_seq((1, DIFF_V_COLS)),
        per_seq((MLA_HEADS, MLA_KV_RANK)), per_seq((MLA_HEADS, MLA_ROPE_DIM)),
        per_seq((1, MLA_KV_RANK)), per_seq((1, MLA_ROPE_DIM)),
        pl.BlockSpec((1, DIFF_V_DIM), lambda b, pt: (0, 0)),
        hbm, hbm, hbm, hbm,
    ]
    caches = (cache_dk, cache_dv, cache_ckv, cache_kr)
    kern = functools.partial(_decode_kernel, pps=pps, n_pages=n_pages, lam_init=lam_init)
    od, om = pl.pallas_call(
        kern,
        grid_spec=pltpu.PrefetchScalarGridSpec(
            num_scalar_prefetch=1,
            grid=(bd,),
            in_specs=in_specs,
            out_specs=[per_seq((1, DIFF_KV_HEADS * DIFF_GROUP * DIFF_V_DIM)),
                       per_seq((1, MLA_HEADS * MLA_KV_RANK))],
            scratch_shapes=[pltpu.VMEM((2, pps) + c.shape[1:], c.dtype) for c in caches] + [
                pltpu.SemaphoreType.DMA((len(caches), 2)),
                pltpu.VMEM((DIFF_KV_HEADS, 8, 1), F32), pltpu.VMEM((DIFF_KV_HEADS, 8, 1), F32),
                pltpu.VMEM((DIFF_KV_HEADS, 8, DIFF_V_DIM), F32),
                pltpu.VMEM((8, 1), F32), pltpu.VMEM((8, 1), F32), pltpu.VMEM((8, MLA_KV_RANK), F32),
            ],
        ),
        out_shape=[jax.ShapeDtypeStruct((bd, 1, DIFF_KV_HEADS * DIFF_GROUP * DIFF_V_DIM), BF16),
                   jax.ShapeDtypeStruct((bd, 1, MLA_HEADS * MLA_KV_RANK), BF16)],
        compiler_params=_cparams(("arbitrary",)),
        name="decode",
    )(pt_flat, lam, qblk, kd_new, vd_new, ql, qr, ckv_new, kr_new, subln_g, *caches)
    return od.reshape(bd, -1), om.reshape(bd, -1)


def _layer_norm(y, g, b):
    yc = y - jnp.mean(y, axis=-1, keepdims=True)
    var = jnp.mean(yc * yc, axis=-1, keepdims=True)
    return yc * lax.rsqrt(var + LN_EPS) * g + b


def _out_ln1_kernel(x_ref, od_ref, oml_ref, wv_ref, wo_ref, g_ref, b_ref, x1_ref, x1b_ref, mix_s, *, alpha):
    half = od_ref.shape[1]
    mix_s[:, :half] = od_ref[...]
    for h in range(MLA_HEADS):
        om = jnp.dot(oml_ref[:, h * MLA_KV_RANK:(h + 1) * MLA_KV_RANK], wv_ref[h], preferred_element_type=F32)
        mix_s[:, half + h * MLA_V_DIM:half + (h + 1) * MLA_V_DIM] = om.astype(BF16)
    y = alpha * x_ref[...] + jnp.dot(mix_s[...], wo_ref[...], preferred_element_type=F32)
    x1 = _layer_norm(y, g_ref[...], b_ref[...])
    x1_ref[...] = x1
    x1b_ref[...] = x1.astype(BF16)


def _out_ln1(x2d, od, oml, wv, wo, g, b, tm, alpha):
    t = x2d.shape[0]
    row = lambda w: pl.BlockSpec((tm, w), lambda i: (i, 0))
    full = lambda a: pl.BlockSpec(a.shape, lambda i: (0,) * a.ndim)
    return pl.pallas_call(
        functools.partial(_out_ln1_kernel, alpha=alpha),
        grid=(t // tm,),
        in_specs=[row(D_MODEL), row(od.shape[1]), row(oml.shape[1]), full(wv), full(wo), full(g), full(b)],
        out_specs=[row(D_MODEL), row(D_MODEL)],
        out_shape=[jax.ShapeDtypeStruct((t, D_MODEL), F32), jax.ShapeDtypeStruct((t, D_MODEL), BF16)],
        scratch_shapes=[pltpu.VMEM((tm, D_MODEL), BF16)],
        compiler_params=_cparams(("parallel",)),
        name="out_ln1",
    )(x2d, od, oml, wv, wo, g, b)


def _peer_query_kernel(x1b_ref, wq_ref, qg_ref, sk_ref, s2_ref, ca_ref, ea_ref, eb_ref,
                       t1_s, t2_s, cand_s):
    tm = x1b_ref.shape[0]
    q = jnp.dot(x1b_ref[...], wq_ref[...], preferred_element_type=F32)
    half = PEER_KEY_DIM // 2
    neg = -jnp.inf

    def top_values(s, t_s):
        work = s
        for i in range(PEER_TOPK):
            m = jnp.max(work, axis=0, keepdims=True)
            t_s[i:i + 1, :] = m
            work = jnp.where(work == m, neg, work)

    for h in range(PEER_HEADS):
        qh = q[:, h * PEER_KEY_DIM:(h + 1) * PEER_KEY_DIM]
        qn = (qh * lax.rsqrt(jnp.mean(qh * qh, axis=-1, keepdims=True) + RMS_EPS) * qg_ref[h]).astype(BF16)
        s1 = lax.dot_general(sk_ref[h, 0], qn[:, :half], NT_DIMS, preferred_element_type=F32)
        s2 = lax.dot_general(sk_ref[h, 1], qn[:, half:], NT_DIMS, preferred_element_type=F32)
        top_values(s1, t1_s)
        top_values(s2, t2_s)
        t2 = t2_s[...]
        row0 = 0
        for i in range(PEER_TOPK):
            n = PEER_TOPK // (i + 1)
            cand_s[row0:row0 + n, :] = t1_s[i:i + 1, :] + t2_s[0:n, :]
            row0 += n
        cand_s[row0:, :] = jnp.full((cand_s.shape[0] - row0, tm), neg, F32)
        work = cand_s[...]
        thr = None
        for i in range(PEER_TOPK):
            thr = jnp.max(work, axis=0, keepdims=True)
            work = jnp.where(work == thr, neg, work)
        top = t1_s[0:1, :] + t2_s[0:1, :]
        cand = cand_s[...]
        z = jnp.sum(jnp.where(cand >= thr, jnp.exp(cand - top), 0.0), axis=0, keepdims=True)
        ca = jnp.full((PEER_N_KEYS, tm), jnp.inf, F32)
        for i in range(PEER_TOPK):
            t1i = t1_s[i:i + 1, :]
            ci = jnp.min(jnp.where(t1i + t2 >= thr, t2, jnp.inf), axis=0, keepdims=True)
            ca = jnp.where(s1 == t1i, ci, ca)
        s2_ref[h] = s2
        ca_ref[h] = ca
        ea_ref[h] = jnp.exp(s1 - t1_s[0:1, :]) / z
        eb_ref[h] = jnp.exp(s2 - t2_s[0:1, :])


def _peer_query(x1b, wq, qg, subkeys, tm):
    t = x1b.shape[0]
    full = lambda a: pl.BlockSpec(a.shape, lambda i: (0,) * a.ndim)
    kt = pl.BlockSpec((PEER_HEADS, PEER_N_KEYS, tm), lambda i: (0, 0, i))
    shp = jax.ShapeDtypeStruct((PEER_HEADS, PEER_N_KEYS, t), F32)
    return pl.pallas_call(
        _peer_query_kernel,
        grid=(t // tm,),
        in_specs=[pl.BlockSpec((tm, D_MODEL), lambda i: (i, 0)), full(wq), full(qg), full(subkeys)],
        out_specs=[kt, kt, kt, kt],
        out_shape=[shp, shp, shp, shp],
        scratch_shapes=[pltpu.VMEM((PEER_TOPK, tm), F32), pltpu.VMEM((PEER_TOPK, tm), F32),
                        pltpu.VMEM((PEER_CAND_ROWS, tm), F32)],
        compiler_params=_cparams(("parallel",)),
        name="peer_query",
    )(x1b, wq, qg, subkeys)


def _gelu(x):
    return 0.5 * x * (1.0 + lax.erf(x * (2.0 ** -0.5)))


def _peer_dense_kernel(x1b_ref, u_ref, vt_ref, s2_ref, eb_ref, ca_ref, ea_ref, f_ref, w_s, acc_s):
    j = pl.program_id(1)
    te = u_ref.shape[0]
    tm = x1b_ref.shape[0]

    @pl.when(j == 0)
    def _():
        acc_s[...] = jnp.zeros(acc_s.shape, F32)

    ht = lax.dot_general(u_ref[...], x1b_ref[...], NT_DIMS, preferred_element_type=F32)

    w_s[...] = _gelu(ht)

    def per_key(k, carry):
        ca = [ca_ref[h, pl.ds(k, 1), :] for h in range(PEER_HEADS)]
        ea = [ea_ref[h, pl.ds(k, 1), :] for h in range(PEER_HEADS)]
        base = pl.multiple_of(k * PEER_N_KEYS, PEER_N_KEYS)
        for bc in range(PEER_N_KEYS // 8):
            g = jnp.zeros((8, tm), F32)
            for h in range(PEER_HEADS):
                sel = s2_ref[h, bc * 8:(bc + 1) * 8, :] >= ca[h]
                g = g + jnp.where(sel, eb_ref[h, bc * 8:(bc + 1) * 8, :], 0.0) * ea[h]
            rows = pl.ds(base + bc * 8, 8)
            w_s[rows, :] = w_s[rows, :] * g
        return carry

    lax.fori_loop(0, te // PEER_N_KEYS, per_key, 0)
    acc_s[...] += jnp.dot(vt_ref[...], w_s[...].astype(BF16), preferred_element_type=F32)

    @pl.when(j == pl.num_programs(1) - 1)
    def _():
        f_ref[...] = acc_s[...].T


def _peer_dense(x1b, u_b, vt_b, s2t, cat, eat, ebt, tm, te):
    t = x1b.shape[0]
    n_exp = u_b.shape[0]
    ka = te // PEER_N_KEYS
    tok = pl.BlockSpec((PEER_HEADS, PEER_N_KEYS, tm), lambda i, j: (0, 0, i))
    key = pl.BlockSpec((PEER_HEADS, ka, tm), lambda i, j: (0, j, i))
    return pl.pallas_call(
        _peer_dense_kernel,
        grid=(t // tm, n_exp // te),
        in_specs=[
            pl.BlockSpec((tm, D_MODEL), lambda i, j: (i, 0)),
            pl.BlockSpec((te, D_MODEL), lambda i, j: (j, 0)),
            pl.BlockSpec((D_MODEL, te), lambda i, j: (0, j)),
            tok, tok, key, key,
        ],
        out_specs=pl.BlockSpec((tm, D_MODEL), lambda i, j: (i, 0)),
        out_shape=jax.ShapeDtypeStruct((t, D_MODEL), F32),
        scratch_shapes=[pltpu.VMEM((te, tm), F32), pltpu.VMEM((D_MODEL, tm), F32)],
        compiler_params=_cparams(("parallel", "arbitrary")),
        name="peer_dense",
    )(x1b, u_b, vt_b, s2t, ebt, cat, eat)


def _ln2_kernel(x1_ref, f_ref, g_ref, b_ref, o_ref, *, alpha):
    o_ref[...] = _layer_norm(alpha * x1_ref[...] + f_ref[...], g_ref[...], b_ref[...])


def _ln2(x1, f, g, b, tm, alpha):
    t = x1.shape[0]
    row = pl.BlockSpec((tm, D_MODEL), lambda i: (i, 0))
    vec = pl.BlockSpec((1, D_MODEL), lambda i: (0, 0))
    return pl.pallas_call(
        functools.partial(_ln2_kernel, alpha=alpha),
        grid=(t // tm,),
        in_specs=[row, row, vec, vec],
        out_specs=row,
        out_shape=jax.ShapeDtypeStruct((t, D_MODEL), F32),
        compiler_params=_cparams(("parallel",)),
        name="ln2",
    )(x1, f, g, b)


def _rope_tables(pos):
    half = DIFF_HEAD_DIM // 2
    inv_freq = ROPE_THETA ** (-jnp.arange(half, dtype=F32) / half)
    ang = pos.astype(F32)[:, None] * inv_freq[None, :]
    cos = jnp.cos(ang)
    sin = jnp.sin(ang)
    return jnp.tile(cos, (1, 4)), jnp.tile(jnp.concatenate([-sin, sin], axis=-1), (1, 2))


def _pick_tile(t, pref):
    tm = min(t, pref)
    assert t % tm == 0, (t, tm)
    return tm


def kernel(x_prompt, x_sample, cache_diff_k, cache_diff_v, cache_mla_ckv, cache_mla_krope, page_table, w_in, diff_lambda_q1, diff_lambda_k1, diff_lambda_q2, diff_lambda_k2, diff_subln_g, mla_q_norm_g, mla_kv_norm_g, w_uq, w_ukv, w_out, ln1_g, ln1_b, ln2_g, ln2_b, peer_wq, peer_q_norm_g, peer_subkeys, peer_u, peer_v):
    depth = w_in.shape[0]
    assert depth == 1, "single-layer trunk"
    layer = 0
    batch, seq, d = x_prompt.shape
    bd, dec_seq, _ = x_sample.shape
    assert d == D_MODEL and dec_seq == 1
    n_pool, page = cache_diff_k.shape[1], cache_diff_k.shape[2]
    past_len = page_table.shape[1] * page
    alpha = (2 * depth) ** 0.25
    lam_init = 0.8 - 0.6 * math.exp(-0.3 * layer)

    lam = (jnp.exp(jnp.sum(diff_lambda_q1[layer] * diff_lambda_k1[layer]))
           - jnp.exp(jnp.sum(diff_lambda_q2[layer] * diff_lambda_k2[layer])) + lam_init).reshape(1).astype(F32)

    w_in_l = w_in[layer]
    w_in_p = jnp.concatenate([w_in_l, w_in_l[:, OFF_KR:OFF_KR + MLA_ROPE_DIM]], axis=1).astype(BF16)
    w_uq_l = w_uq[layer]
    w_uq_p = jnp.concatenate([
        w_uq_l[:, :, :MLA_NOPE_DIM].reshape(MLA_Q_RANK, -1),
        w_uq_l[:, :, MLA_NOPE_DIM:].reshape(MLA_Q_RANK, -1)], axis=1).astype(BF16)
    w_ukv_l = w_ukv[layer]
    wn = w_ukv_l[:, :, :MLA_NOPE_DIM].transpose(1, 2, 0).astype(BF16)
    wv = w_ukv_l[:, :, MLA_NOPE_DIM:].transpose(1, 0, 2).astype(BF16)
    wo = w_out[layer].astype(BF16)
    wq = peer_wq[layer].astype(BF16)
    subkeys = peer_subkeys[layer].astype(BF16)
    u_b = peer_u[layer].astype(BF16)
    vt_b = peer_v[layer].T.astype(BF16)
    qg = mla_q_norm_g[layer].reshape(1, -1)
    kvg = mla_kv_norm_g[layer].reshape(1, -1)
    subln = diff_subln_g[layer].reshape(1, -1)
    pqg = peer_q_norm_g[layer].reshape(PEER_HEADS, 1, PEER_KEY_DIM)
    g1, b1 = ln1_g[layer].reshape(1, -1), ln1_b[layer].reshape(1, -1)
    g2, b2 = ln2_g[layer].reshape(1, -1), ln2_b[layer].reshape(1, -1)

    def project(x2d, pos, tm):
        cos, sin = _rope_tables(pos)
        return _proj(x2d, cos, sin, w_in_p, qg, w_uq_p, kvg, wn, tm)

    def block_out(x2d, od, oml, tm, tm_peer, te):
        x1, x1b = _out_ln1(x2d, od, oml, wv, wo, g1, b1, tm, alpha)
        s2t, cat, eat, ebt = _peer_query(x1b, wq, pqg, subkeys, tm)
        f = _peer_dense(x1b, u_b, vt_b, s2t, cat, eat, ebt, tm_peer, te)
        return _ln2(x1, f, g2, b2, tm, alpha)

    tp = batch * seq
    xp = x_prompt.reshape(tp, d)
    tm_p = _pick_tile(tp, 256)
    pos_p = jnp.tile(jnp.arange(seq, dtype=jnp.int32), batch)
    qd, kd, vd, ckv, kr, kdb, kcat, qlat, qrope, vdt, ckvt = project(xp, pos_p, tm_p)
    tq = _pick_tile(seq, 256)
    od = _diff_flash(lam, qd, kdb, vdt, subln.reshape(-1, 1), batch, seq, tq, lam_init)
    oml = _mla_flash(qlat, qrope, kcat, ckvt, batch, seq, tq)
    y_p = block_out(xp, od, oml, tm_p, _pick_tile(tp, 512), 1024).reshape(batch, seq, d)
    p_dk = kd.reshape(1, batch, seq, DIFF_KV_HEADS, 2, DIFF_HEAD_DIM)
    p_dv = vd.reshape(1, batch, seq, DIFF_KV_HEADS, DIFF_V_DIM)
    p_ckv = ckv.reshape(1, batch, seq, MLA_KV_RANK)
    p_kr = kr.reshape(1, batch, seq, MLA_ROPE_DIM)

    xs = x_sample.reshape(bd, d)
    tm_s = _pick_tile(bd, 128)
    pos_s = jnp.full((bd,), past_len, jnp.int32)
    qd, kd, vd, ckv, kr, _, _, qlat, qrope, _, _ = project(xs, pos_s, tm_s)
    q5 = qd.reshape(bd, DIFF_KV_HEADS, DIFF_GROUP, 2, DIFF_HEAD_DIM)
    zeros = jnp.zeros_like(q5[:, :, :, 0])
    qblk = jnp.concatenate([
        jnp.concatenate([q5[:, :, :, 0], zeros], axis=-1),
        jnp.concatenate([zeros, q5[:, :, :, 1]], axis=-1)], axis=2)
    od_s, oml_s = _decode(
        page_table, lam, qblk, kd.reshape(bd, 1, -1), vd.reshape(bd, 1, -1),
        qlat.reshape(bd, MLA_HEADS, MLA_KV_RANK), qrope.reshape(bd, MLA_HEADS, MLA_ROPE_DIM),
        ckv.reshape(bd, 1, -1), kr.reshape(bd, 1, -1), subln,
        cache_diff_k.transpose(0, 1, 3, 4, 5, 2).reshape(depth * n_pool, DIFF_K_COLS, page),
        cache_diff_v.reshape(depth * n_pool, page * DIFF_KV_HEADS, DIFF_V_DIM),
        cache_mla_ckv.reshape(depth * n_pool, page, MLA_KV_RANK),
        cache_mla_krope.transpose(0, 1, 3, 2).reshape(depth * n_pool, MLA_ROPE_DIM, page),
        pps=_pick_tile(page_table.shape[1], 16), lam_init=lam_init)
    y_s = block_out(xs, od_s, oml_s, tm_s, tm_s, 1024).reshape(bd, 1, d)
    s_dk = kd.reshape(1, bd, 1, DIFF_KV_HEADS, 2, DIFF_HEAD_DIM)
    s_dv = vd.reshape(1, bd, 1, DIFF_KV_HEADS, DIFF_V_DIM)
    s_ckv = ckv.reshape(1, bd, 1, MLA_KV_RANK)
    s_kr = kr.reshape(1, bd, 1, MLA_ROPE_DIM)

    return (y_p, y_s, p_dk, p_dv, p_ckv, p_kr, s_dk, s_dv, s_ckv, s_kr)
```

```python
import functools
import math

import jax
import jax.numpy as jnp
from jax import lax
from jax.experimental import pallas as pl
from jax.experimental.pallas import tpu as pltpu

BF16 = jnp.bfloat16
F32 = jnp.float32

D_MODEL = 2048
DIFF_HEAD_DIM = 64
DIFF_V_DIM = 128
DIFF_KV_HEADS = 2
DIFF_GROUP = 4
DIFF_SCALE = DIFF_HEAD_DIM ** -0.5
MLA_NOPE_DIM = 128
MLA_ROPE_DIM = 64
MLA_V_DIM = 128
MLA_HEADS = 8
MLA_Q_RANK = 512
MLA_KV_RANK = 256
MLA_SCALE = (MLA_NOPE_DIM + MLA_ROPE_DIM) ** -0.5
PEER_HEADS = 8
PEER_KEY_DIM = 256
PEER_N_KEYS = 128
PEER_TOPK = 16
PEER_CAND_ROWS = -(-sum(PEER_TOPK // (i + 1) for i in range(PEER_TOPK)) // 8) * 8
ROPE_THETA = 10000.0
LN_EPS = 1e-5
RMS_EPS = 1e-6

DIFF_Q_COLS = 1024
DIFF_K_COLS = 256
DIFF_V_COLS = 256
OFF_K = DIFF_Q_COLS
OFF_V = OFF_K + DIFF_K_COLS
OFF_CQ = OFF_V + DIFF_V_COLS
OFF_CKV = OFF_CQ + MLA_Q_RANK
OFF_KR = OFF_CKV + MLA_KV_RANK
D_IN_PAD = OFF_KR + 128

LANES = 128
ROWS_BF16 = 16
VMEM_LIMIT = 56 * 1024 * 1024

NT_DIMS = (((1,), (1,)), ((), ()))


def _cparams(sem):
    return pltpu.CompilerParams(dimension_semantics=sem, vmem_limit_bytes=VMEM_LIMIT)


def _rope_chunk(c, cos, sin_signed, first_half):
    sw = jnp.where(first_half, pltpu.roll(c, 96, 1), pltpu.roll(c, 32, 1))
    return c * cos + sw * sin_signed


def _proj_kernel(x_ref, cos_ref, sin_ref, win_ref, qg_ref, wuq_ref, kvg_ref, wn_ref,
                 qd_ref, kd_ref, vd_ref, ckv_ref, kr_ref, kdb_ref, kcat_ref, qlat_ref, qrope_ref,
                 vdt_ref, ckvt_ref):
    tm = x_ref.shape[0]
    z = jnp.dot(x_ref[...].astype(BF16), win_ref[...], preferred_element_type=F32)
    cos = cos_ref[...]
    sin = sin_ref[...]
    lane = lax.broadcasted_iota(jnp.int32, (tm, LANES), 1)
    first = (lane % 64) < 32

    def rope(c):
        return _rope_chunk(c, cos, sin, first)

    for j in range(DIFF_Q_COLS // LANES):
        qd_ref[:, j * LANES:(j + 1) * LANES] = rope(z[:, j * LANES:(j + 1) * LANES]).astype(BF16)
    for j in range(DIFF_K_COLS // LANES):
        k = rope(z[:, OFF_K + j * LANES:OFF_K + (j + 1) * LANES])
        kd_ref[:, j * LANES:(j + 1) * LANES] = k
        kdb_ref[:, j * LANES:(j + 1) * LANES] = k.astype(BF16)
    v = z[:, OFF_V:OFF_V + DIFF_V_COLS]
    vd_ref[...] = v
    vdt_ref[...] = v.T.astype(BF16)

    ckv = z[:, OFF_CKV:OFF_CKV + MLA_KV_RANK]
    ckv = ckv * lax.rsqrt(jnp.mean(ckv * ckv, axis=-1, keepdims=True) + RMS_EPS) * kvg_ref[...]
    ckv_ref[...] = ckv
    kcat_ref[:, :MLA_KV_RANK] = ckv.astype(BF16)
    ckvt_ref[...] = ckv.T.astype(BF16)
    kr2 = rope(z[:, OFF_KR:OFF_KR + LANES])
    kr_ref[...] = kr2[:, :MLA_ROPE_DIM]
    kcat_ref[:, MLA_KV_RANK:] = kr2.astype(BF16)

    cq = z[:, OFF_CQ:OFF_CQ + MLA_Q_RANK]
    cq = cq * lax.rsqrt(jnp.mean(cq * cq, axis=-1, keepdims=True) + RMS_EPS) * qg_ref[...]
    q = jnp.dot(cq.astype(BF16), wuq_ref[...], preferred_element_type=F32)
    nope_cols = MLA_HEADS * MLA_NOPE_DIM
    for j in range(MLA_HEADS * MLA_ROPE_DIM // LANES):
        qrope_ref[:, j * LANES:(j + 1) * LANES] = rope(
            q[:, nope_cols + j * LANES:nope_cols + (j + 1) * LANES]).astype(BF16)
    for h in range(MLA_HEADS):
        qn = q[:, h * MLA_NOPE_DIM:(h + 1) * MLA_NOPE_DIM].astype(BF16)
        qlat_ref[:, h * MLA_KV_RANK:(h + 1) * MLA_KV_RANK] = jnp.dot(
            qn, wn_ref[h], preferred_element_type=F32).astype(BF16)


def _proj(x2d, cos, sin, w_in_p, q_norm_g, w_uq_p, kv_norm_g, wn, tm):
    t = x2d.shape[0]
    row = lambda w: pl.BlockSpec((tm, w), lambda i: (i, 0))
    full = lambda a: pl.BlockSpec(a.shape, lambda i: (0,) * a.ndim)
    outs = [
        (DIFF_Q_COLS, BF16), (DIFF_K_COLS, F32), (DIFF_V_COLS, F32), (MLA_KV_RANK, F32),
        (MLA_ROPE_DIM, F32), (DIFF_K_COLS, BF16), (MLA_KV_RANK + LANES, BF16),
        (MLA_HEADS * MLA_KV_RANK, BF16), (MLA_HEADS * MLA_ROPE_DIM, BF16),
    ]
    col = lambda h: pl.BlockSpec((h, tm), lambda i: (0, i))
    return pl.pallas_call(
        _proj_kernel,
        grid=(t // tm,),
        in_specs=[row(D_MODEL), row(LANES), row(LANES), full(w_in_p), full(q_norm_g), full(w_uq_p),
                  full(kv_norm_g), full(wn)],
        out_specs=[row(w) for w, _ in outs] + [col(DIFF_V_COLS), col(MLA_KV_RANK)],
        out_shape=[jax.ShapeDtypeStruct((t, w), dt) for w, dt in outs]
        + [jax.ShapeDtypeStruct((DIFF_V_COLS, t), BF16), jax.ShapeDtypeStruct((MLA_KV_RANK, t), BF16)],
        compiler_params=_cparams(("parallel",)),
        name="proj",
    )(x2d, cos, sin, w_in_p, q_norm_g, w_uq_p, kv_norm_g, wn)


def _online_update(st, vt, m_s, l_s, acc_s):
    m_prev = m_s[...]
    m_new = jnp.maximum(m_prev, jnp.max(st, axis=0, keepdims=True))
    corr = jnp.exp(m_prev - m_new)
    p = jnp.exp(st - m_new)
    l_s[...] = l_s[...] * corr + jnp.sum(p, axis=0, keepdims=True)
    acc_s[...] = acc_s[...] * corr + jnp.dot(vt, p.astype(BF16), preferred_element_type=F32)
    m_s[...] = m_new


def _causal_mask(st, tq):
    keys = lax.broadcasted_iota(jnp.int32, st.shape, 0)
    queries = lax.broadcasted_iota(jnp.int32, st.shape, 1) & (tq - 1)
    return jnp.where(keys <= queries, st, -jnp.inf)


def _causal_pairs(nq):
    pairs = [(i, j) for i in range(nq) for j in range(i + 1)]
    return (jnp.array([p[0] for p in pairs], jnp.int32), jnp.array([p[1] for p in pairs], jnp.int32))


def _diff_flash_kernel(qt_ref, kt_ref, lam_ref, q_ref, k_ref, v_ref, g_ref, o_ref, qs, m_s, l_s, acc_s,
                       *, tq, lam_init):
    qi = qt_ref[pl.program_id(2)]
    kj = kt_ref[pl.program_id(2)]

    @pl.when(kj == 0)
    def _():
        lane = lax.broadcasted_iota(jnp.int32, (tq, LANES), 1)
        zero = jnp.zeros((tq, LANES), BF16)
        for g in range(DIFF_GROUP):
            chunk = q_ref[:, g * LANES:(g + 1) * LANES]
            qs[g * tq:(g + 1) * tq, :] = jnp.where(lane < 64, chunk, zero)
            qs[(DIFF_GROUP + g) * tq:(DIFF_GROUP + g + 1) * tq, :] = jnp.where(lane >= 64, chunk, zero)
        m_s[...] = jnp.full(m_s.shape, -jnp.inf, F32)
        l_s[...] = jnp.zeros(l_s.shape, F32)
        acc_s[...] = jnp.zeros(acc_s.shape, F32)

    def scores():
        return lax.dot_general(k_ref[...], qs[...], NT_DIMS, preferred_element_type=F32) * DIFF_SCALE

    @pl.when(kj < qi)
    def _():
        _online_update(scores(), v_ref[...], m_s, l_s, acc_s)

    @pl.when(kj == qi)
    def _():
        _online_update(_causal_mask(scores(), tq), v_ref[...], m_s, l_s, acc_s)
        o = acc_s[...] / l_s[...]
        d = o[:, :DIFF_GROUP * tq] - lam_ref[0] * o[:, DIFF_GROUP * tq:]
        d = d * lax.rsqrt(jnp.mean(d * d, axis=0, keepdims=True) + RMS_EPS) * g_ref[...] * (1.0 - lam_init)
        for g in range(DIFF_GROUP):
            o_ref[:, g * DIFF_V_DIM:(g + 1) * DIFF_V_DIM] = d[:, g * tq:(g + 1) * tq].T.astype(BF16)


def _diff_flash(lam, qd, kdb, vdt, subln_col, batch, seq, tq, lam_init):
    nq = seq // tq
    ngc = 2 * DIFF_GROUP
    kern = functools.partial(_diff_flash_kernel, tq=tq, lam_init=lam_init)
    gw = DIFF_GROUP * 2 * DIFF_HEAD_DIM
    qt, kt = _causal_pairs(nq)
    return pl.pallas_call(
        kern,
        grid_spec=pltpu.PrefetchScalarGridSpec(
            num_scalar_prefetch=2,
            grid=(batch, DIFF_KV_HEADS, qt.shape[0]),
            in_specs=[
                pl.BlockSpec(memory_space=pltpu.SMEM),
                pl.BlockSpec((tq, gw), lambda b, h, t, qt, kt: (b * nq + qt[t], h)),
                pl.BlockSpec((tq, LANES), lambda b, h, t, qt, kt: (b * nq + kt[t], h)),
                pl.BlockSpec((DIFF_V_DIM, tq), lambda b, h, t, qt, kt: (h, b * nq + kt[t])),
                pl.BlockSpec((DIFF_V_DIM, 1), lambda b, h, t, qt, kt: (0, 0)),
            ],
            out_specs=pl.BlockSpec((tq, DIFF_GROUP * DIFF_V_DIM), lambda b, h, t, qt, kt: (b * nq + qt[t], h)),
            scratch_shapes=[
                pltpu.VMEM((ngc * tq, LANES), BF16),
                pltpu.VMEM((1, ngc * tq), F32),
                pltpu.VMEM((1, ngc * tq), F32),
                pltpu.VMEM((DIFF_V_DIM, ngc * tq), F32),
            ],
        ),
        out_shape=jax.ShapeDtypeStruct((batch * seq, DIFF_KV_HEADS * DIFF_GROUP * DIFF_V_DIM), BF16),
        compiler_params=_cparams(("parallel", "parallel", "arbitrary")),
        name="diff_flash",
    )(qt, kt, lam, qd, kdb, vdt, subln_col)


def _mla_flash_kernel(qt_ref, kt_ref, ql_ref, qr_ref, k_ref, vt_ref, o_ref, qs, m_s, l_s, acc_s, *, tq):
    qi = qt_ref[pl.program_id(1)]
    kj = kt_ref[pl.program_id(1)]

    @pl.when(kj == 0)
    def _():
        lane = lax.broadcasted_iota(jnp.int32, (tq, LANES), 1)
        zero = jnp.zeros((tq, LANES), BF16)
        for h in range(MLA_HEADS):
            qs[h * tq:(h + 1) * tq, :MLA_KV_RANK] = ql_ref[:, h * MLA_KV_RANK:(h + 1) * MLA_KV_RANK]
            chunk = qr_ref[:, (h // 2) * LANES:(h // 2 + 1) * LANES]
            keep = (lane < 64) if h % 2 == 0 else (lane >= 64)
            qs[h * tq:(h + 1) * tq, MLA_KV_RANK:] = jnp.where(keep, chunk, zero)
        m_s[...] = jnp.full(m_s.shape, -jnp.inf, F32)
        l_s[...] = jnp.zeros(l_s.shape, F32)
        acc_s[...] = jnp.zeros(acc_s.shape, F32)

    def scores():
        return lax.dot_general(k_ref[...], qs[...], NT_DIMS, preferred_element_type=F32) * MLA_SCALE

    @pl.when(kj < qi)
    def _():
        _online_update(scores(), vt_ref[...], m_s, l_s, acc_s)

    @pl.when(kj == qi)
    def _():
        _online_update(_causal_mask(scores(), tq), vt_ref[...], m_s, l_s, acc_s)
        o = acc_s[...] / l_s[...]
        for h in range(MLA_HEADS):
            o_ref[:, h * MLA_KV_RANK:(h + 1) * MLA_KV_RANK] = o[:, h * tq:(h + 1) * tq].T.astype(BF16)


def _mla_flash(qlat, qrope, kcat, ckvt, batch, seq, tq):
    nq = seq // tq
    kern = functools.partial(_mla_flash_kernel, tq=tq)
    kw = MLA_KV_RANK + LANES
    qt, kt = _causal_pairs(nq)
    return pl.pallas_call(
        kern,
        grid_spec=pltpu.PrefetchScalarGridSpec(
            num_scalar_prefetch=2,
            grid=(batch, qt.shape[0]),
            in_specs=[
                pl.BlockSpec((tq, MLA_HEADS * MLA_KV_RANK), lambda b, t, qt, kt: (b * nq + qt[t], 0)),
                pl.BlockSpec((tq, MLA_HEADS * MLA_ROPE_DIM), lambda b, t, qt, kt: (b * nq + qt[t], 0)),
                pl.BlockSpec((tq, kw), lambda b, t, qt, kt: (b * nq + kt[t], 0)),
                pl.BlockSpec((MLA_KV_RANK, tq), lambda b, t, qt, kt: (0, b * nq + kt[t])),
            ],
            out_specs=pl.BlockSpec((tq, MLA_HEADS * MLA_KV_RANK), lambda b, t, qt, kt: (b * nq + qt[t], 0)),
            scratch_shapes=[
                pltpu.VMEM((MLA_HEADS * tq, kw), BF16),
                pltpu.VMEM((1, MLA_HEADS * tq), F32),
                pltpu.VMEM((1, MLA_HEADS * tq), F32),
                pltpu.VMEM((MLA_KV_RANK, MLA_HEADS * tq), F32),
            ],
        ),
        out_shape=jax.ShapeDtypeStruct((batch * seq, MLA_HEADS * MLA_KV_RANK), BF16),
        compiler_params=_cparams(("parallel", "arbitrary")),
        name="mla_flash",
    )(qt, kt, qlat, qrope, kcat, ckvt)


def _decode_kernel(pt_ref, lam_ref, qd_ref, kdn_ref, vdn_ref, ql_ref, qr_ref, ckvn_ref, krn_ref, g_ref,
                   dk_hbm, dv_hbm, ck_hbm, kr_hbm, od_ref, om_ref,
                   dk_buf, dv_buf, ck_buf, kr_buf, sems, md_s, ld_s, accd_s, mm_s, lm_s, accm_s,
                   *, pps, n_pages, lam_init):
    b = pl.program_id(0)
    n_steps = n_pages // pps
    page = ck_buf.shape[2]
    hbm_bufs =((dk_hbm, dk_buf), (dv_hbm, dv_buf), (ck_hbm, ck_buf), (kr_hbm, kr_buf))

    def page_copy(a, page_id, slot, p):
        hbm, buf = hbm_bufs[a]
        return pltpu.make_async_copy(hbm.at[page_id], buf.at[slot, p], sems.at[a, slot])

    def start_fetch(seq, step, slot):
        for p in range(pps):
            page_id = pt_ref[seq * n_pages + step * pps + p]
            for a in range(len(hbm_bufs)):
                page_copy(a, page_id, slot, p).start()

    def wait_fetch(slot):
        for p in range(pps):
            for a in range(len(hbm_bufs)):
                page_copy(a, 0, slot, p).wait()

    @pl.when(b == 0)
    def _():
        start_fetch(0, 0, 0)

    def init():
        for h in range(DIFF_KV_HEADS):
            q = qd_ref[0, h].astype(F32)
            kn = kdn_ref[0, :, h * LANES:(h + 1) * LANES]
            md_s[h] = jnp.sum(q * kn.astype(BF16).astype(F32), axis=-1, keepdims=True) * DIFF_SCALE
            ld_s[h] = jnp.ones((8, 1), F32)
            accd_s[h] = jnp.broadcast_to(vdn_ref[0, :, h * LANES:(h + 1) * LANES].astype(BF16).astype(F32),
                                         (8, DIFF_V_DIM))
        ql = ql_ref[0].astype(F32)
        qr = qr_ref[0].astype(F32)
        cn = ckvn_ref[0].astype(BF16).astype(F32)
        rn = krn_ref[0].astype(BF16).astype(F32)
        mm_s[...] = (jnp.sum(ql * cn, axis=-1, keepdims=True)
                     + jnp.sum(qr * rn, axis=-1, keepdims=True)) * MLA_SCALE
        lm_s[...] = jnp.ones((8, 1), F32)
        accm_s[...] = jnp.broadcast_to(cn, (8, MLA_KV_RANK))

    def merge(s_list, v_list, m_ref, l_ref, acc_ref):
        s = jnp.concatenate(s_list, axis=-1)
        m_prev = m_ref[...]
        m_new = jnp.maximum(m_prev, jnp.max(s, axis=-1, keepdims=True))
        corr = jnp.exp(m_prev - m_new)
        p = jnp.exp(s - m_new)
        l_ref[...] = l_ref[...] * corr + jnp.sum(p, axis=-1, keepdims=True)
        pb = p.astype(BF16)
        acc = acc_ref[...] * corr
        for i, v in enumerate(v_list):
            acc = acc + jnp.dot(pb[:, i * LANES:(i + 1) * LANES], v, preferred_element_type=F32)
        acc_ref[...] = acc
        m_ref[...] = m_new

    def compute(slot):
        for h in range(DIFF_KV_HEADS):
            q = qd_ref[0, h]
            s_list = [
                jnp.dot(q, dk_buf[slot, p, h * LANES:(h + 1) * LANES, :].astype(BF16),
                        preferred_element_type=F32) * DIFF_SCALE
                for p in range(pps)]
            vs = [dv_buf[slot, p, pl.ds(h, page, stride=DIFF_KV_HEADS), :].astype(BF16) for p in range(pps)]
            merge(s_list, vs, md_s.at[h], ld_s.at[h], accd_s.at[h])
        ql = ql_ref[0]
        qr = qr_ref[0]
        cs = [ck_buf[slot, p].astype(BF16) for p in range(pps)]
        s_list = [
            (lax.dot_general(ql, cs[p], NT_DIMS, preferred_element_type=F32)
             + jnp.dot(qr, kr_buf[slot, p].astype(BF16), preferred_element_type=F32)) * MLA_SCALE
            for p in range(pps)]
        merge(s_list, cs, mm_s, lm_s, accm_s)

    def body(step, carry):
        slot = (b * n_steps + step) & 1

        @pl.when(step + 1 < n_steps)
        def _():
            start_fetch(b, step + 1, 1 - slot)

        @pl.when(jnp.logical_and(step + 1 == n_steps, b + 1 < pl.num_programs(0)))
        def _():
            start_fetch(b + 1, 0, 1 - slot)

        wait_fetch(slot)
        compute(slot)
        return carry

    init()
    lax.fori_loop(0, n_steps, body, 0)

    def finish():
        for h in range(DIFF_KV_HEADS):
            o = accd_s[h] / ld_s[h]
            d = o[:DIFF_GROUP] - lam_ref[0] * o[DIFF_GROUP:]
            d = d * lax.rsqrt(jnp.mean(d * d, axis=-1, keepdims=True) + RMS_EPS) * g_ref[...] * (1.0 - lam_init)
            for g in range(DIFF_GROUP):
                c0 = (h * DIFF_GROUP + g) * DIFF_V_DIM
                od_ref[0, :, c0:c0 + DIFF_V_DIM] = d[g:g + 1].astype(BF16)
        o = accm_s[...] / lm_s[...]
        for h in range(MLA_HEADS):
            om_ref[0, :, h * MLA_KV_RANK:(h + 1) * MLA_KV_RANK] = o[h:h + 1].astype(BF16)

    finish()


def _decode(page_table, lam, qblk, kd_new, vd_new, ql, qr, ckv_new, kr_new, subln_g,
            cache_dk, cache_dv, cache_ckv, cache_kr, pps, lam_init):
    bd, n_pages = page_table.shape
    pt_flat = page_table.reshape(-1)
    per_seq = lambda shape: pl.BlockSpec((1,) + shape, lambda b, pt: (b,) + (0,) * len(shape))
    hbm = pl.BlockSpec(memory_space=pl.ANY)
    in_specs = [
        pl.BlockSpec(memory_space=pltpu.SMEM),
        per_seq((DIFF_KV_HEADS, 8, LANES)),
        per_seq((1, DIFF_K_COLS)), per_seq((1, DIFF_V_COLS)),
        per_seq((MLA_HEADS, MLA_KV_RANK)), per_seq((MLA_HEADS, MLA_ROPE_DIM)),
        per_seq((1, MLA_KV_RANK)), per_seq((1, MLA_ROPE_DIM)),
        pl.BlockSpec((1, DIFF_V_DIM), lambda b, pt: (0, 0)),
        hbm, hbm, hbm, hbm,
    ]
    caches = (cache_dk, cache_dv, cache_ckv, cache_kr)
    kern = functools.partial(_decode_kernel, pps=pps, n_pages=n_pages, lam_init=lam_init)
    od, om = pl.pallas_call(
        kern,
        grid_spec=pltpu.PrefetchScalarGridSpec(
            num_scalar_prefetch=1,
            grid=(bd,),
            in_specs=in_specs,
            out_specs=[per_seq((1, DIFF_KV_HEADS * DIFF_GROUP * DIFF_V_DIM)),
                       per_seq((1, MLA_HEADS * MLA_KV_RANK))],
            scratch_shapes=[pltpu.VMEM((2, pps) + c.shape[1:], c.dtype) for c in caches] + [
                pltpu.SemaphoreType.DMA((len(caches), 2)),
                pltpu.VMEM((DIFF_KV_HEADS, 8, 1), F32), pltpu.VMEM((DIFF_KV_HEADS, 8, 1), F32),
                pltpu.VMEM((DIFF_KV_HEADS, 8, DIFF_V_DIM), F32),
                pltpu.VMEM((8, 1), F32), pltpu.VMEM((8, 1), F32), pltpu.VMEM((8, MLA_KV_RANK), F32),
            ],
        ),
        out_shape=[jax.ShapeDtypeStruct((bd, 1, DIFF_KV_HEADS * DIFF_GROUP * DIFF_V_DIM), BF16),
                   jax.ShapeDtypeStruct((bd, 1, MLA_HEADS * MLA_KV_RANK), BF16)],
        compiler_params=_cparams(("arbitrary",)),
        name="decode",
    )(pt_flat, lam, qblk, kd_new, vd_new, ql, qr, ckv_new, kr_new, subln_g, *caches)
    return od.reshape(bd, -1), om.reshape(bd, -1)


def _layer_norm(y, g, b):
    yc = y - jnp.mean(y, axis=-1, keepdims=True)
    var = jnp.mean(yc * yc, axis=-1, keepdims=True)
    return yc * lax.rsqrt(var + LN_EPS) * g + b


def _out_ln1_kernel(x_ref, od_ref, oml_ref, wv_ref, wo_ref, g_ref, b_ref, x1_ref, x1b_ref, mix_s, *, alpha):
    half = od_ref.shape[1]
    mix_s[:, :half] = od_ref[...]
    for h in range(MLA_HEADS):
        om = jnp.dot(oml_ref[:, h * MLA_KV_RANK:(h + 1) * MLA_KV_RANK], wv_ref[h], preferred_element_type=F32)
        mix_s[:, half + h * MLA_V_DIM:half + (h + 1) * MLA_V_DIM] = om.astype(BF16)
    y = alpha * x_ref[...] + jnp.dot(mix_s[...], wo_ref[...], preferred_element_type=F32)
    x1 = _layer_norm(y, g_ref[...], b_ref[...])
    x1_ref[...] = x1
    x1b_ref[...] = x1.astype(BF16)


def _out_ln1(x2d, od, oml, wv, wo, g, b, tm, alpha):
    t = x2d.shape[0]
    row = lambda w: pl.BlockSpec((tm, w), lambda i: (i, 0))
    full = lambda a: pl.BlockSpec(a.shape, lambda i: (0,) * a.ndim)
    return pl.pallas_call(
        functools.partial(_out_ln1_kernel, alpha=alpha),
        grid=(t // tm,),
        in_specs=[row(D_MODEL), row(od.shape[1]), row(oml.shape[1]), full(wv), full(wo), full(g), full(b)],
        out_specs=[row(D_MODEL), row(D_MODEL)],
        out_shape=[jax.ShapeDtypeStruct((t, D_MODEL), F32), jax.ShapeDtypeStruct((t, D_MODEL), BF16)],
        scratch_shapes=[pltpu.VMEM((tm, D_MODEL), BF16)],
        compiler_params=_cparams(("parallel",)),
        name="out_ln1",
    )(x2d, od, oml, wv, wo, g, b)


def _peer_query_kernel(x1b_ref, wq_ref, qg_ref, sk_ref, r2_ref, na_ref, ea_ref, eb_ref,
                       t1_s, t2_s, cand_s, s_s):
    tm = x1b_ref.shape[0]
    q = jnp.dot(x1b_ref[...], wq_ref[...], preferred_element_type=F32)
    half = PEER_KEY_DIM // 2
    neg = -jnp.inf

    def top_values(s, t_s, want_rank):
        work = s
        rank = jnp.full(s.shape, float(PEER_TOPK), F32) if want_rank else None
        for i in range(PEER_TOPK):
            m = jnp.max(work, axis=0, keepdims=True)
            t_s[i:i + 1, :] = m
            hit = work == m
            if want_rank:
                rank = jnp.where(hit, float(i), rank)
            work = jnp.where(hit, neg, work)
        return rank

    lw = t1_s.shape[1]

    def select(h, lanes):
        s1 = s_s[0, :, lanes]
        s2 = s_s[1, :, lanes]
        top_values(s1, t1_s, False)
        rank2 = top_values(s2, t2_s, True)
        t2 = t2_s[...]
        row0 = 0
        for i in range(PEER_TOPK):
            n = PEER_TOPK // (i + 1)
            cand_s[row0:row0 + n, :] = t1_s[i:i + 1, :] + t2_s[0:n, :]
            row0 += n
        cand_s[row0:, :] = jnp.full((cand_s.shape[0] - row0, lw), neg, F32)
        work = cand_s[...]
        thr = None
        for i in range(PEER_TOPK):
            thr = jnp.max(work, axis=0, keepdims=True)
            work = jnp.where(work == thr, neg, work)
        top = t1_s[0:1, :] + t2_s[0:1, :]
        cand = cand_s[...]
        z = jnp.sum(jnp.where(cand >= thr, jnp.exp(cand - top), 0.0), axis=0, keepdims=True)
        na = jnp.zeros((PEER_N_KEYS, lw), F32)
        for i in range(PEER_TOPK):
            t1i = t1_s[i:i + 1, :]
            ni = jnp.sum(jnp.where(t1i + t2 >= thr, 1.0, 0.0), axis=0, keepdims=True)
            na = jnp.where(s1 == t1i, ni, na)
        r2_ref[h, :, lanes] = rank2
        na_ref[h, :, lanes] = na
        ea_ref[h, :, lanes] = jnp.exp(s1 - t1_s[0:1, :]) / z
        eb_ref[h, :, lanes] = jnp.exp(s2 - t2_s[0:1, :])

    for h in range(PEER_HEADS):
        qh = q[:, h * PEER_KEY_DIM:(h + 1) * PEER_KEY_DIM]
        qn = (qh * lax.rsqrt(jnp.mean(qh * qh, axis=-1, keepdims=True) + RMS_EPS) * qg_ref[h]).astype(BF16)
        s_s[0] = lax.dot_general(sk_ref[h, 0], qn[:, :half], NT_DIMS, preferred_element_type=F32)
        s_s[1] = lax.dot_general(sk_ref[h, 1], qn[:, half:], NT_DIMS, preferred_element_type=F32)
        for lt in range(tm // lw):
            select(h, slice(lt * lw, (lt + 1) * lw))


def _peer_query(x1b, wq, qg, subkeys, tm):
    t = x1b.shape[0]
    full = lambda a: pl.BlockSpec(a.shape, lambda i: (0,) * a.ndim)
    kt = pl.BlockSpec((PEER_HEADS, PEER_N_KEYS, tm), lambda i: (0, 0, i))
    shp = jax.ShapeDtypeStruct((PEER_HEADS, PEER_N_KEYS, t), F32)
    lw = min(tm, LANES)
    return pl.pallas_call(
        _peer_query_kernel,
        grid=(t // tm,),
        in_specs=[pl.BlockSpec((tm, D_MODEL), lambda i: (i, 0)), full(wq), full(qg), full(subkeys)],
        out_specs=[kt, kt, kt, kt],
        out_shape=[shp, shp, shp, shp],
        scratch_shapes=[pltpu.VMEM((PEER_TOPK, lw), F32), pltpu.VMEM((PEER_TOPK, lw), F32),
                        pltpu.VMEM((PEER_CAND_ROWS, lw), F32), pltpu.VMEM((2, PEER_N_KEYS, tm), F32)],
        compiler_params=_cparams(("parallel",)),
        name="peer_query",
    )(x1b, wq, qg, subkeys)


def _gelu(x):
    return 0.5 * x * (1.0 + lax.erf(x * (2.0 ** -0.5)))


def _peer_dense_kernel(x1b_ref, u_ref, vt_ref, r2_ref, eb_ref, na_ref, ea_ref, f_ref, w_s, acc_s):
    j = pl.program_id(1)
    te = u_ref.shape[0]
    tm = x1b_ref.shape[0]

    @pl.when(j == 0)
    def _():
        acc_s[...] = jnp.zeros(acc_s.shape, F32)

    ht = lax.dot_general(u_ref[...], x1b_ref[...], NT_DIMS, preferred_element_type=F32)
    w_s[...] = _gelu(ht)

    def per_key(k, carry):
        na = [na_ref[h, pl.ds(k, 1), :] for h in range(PEER_HEADS)]
        ea = [ea_ref[h, pl.ds(k, 1), :] for h in range(PEER_HEADS)]
        base = pl.multiple_of(k * PEER_N_KEYS, PEER_N_KEYS)
        for b0 in range(0, PEER_N_KEYS, 8):
            g = jnp.zeros((8, tm), F32)
            for h in range(PEER_HEADS):
                sel = r2_ref[h, b0:b0 + 8, :] < na[h]
                g = g + jnp.where(sel, eb_ref[h, b0:b0 + 8, :], 0.0) * ea[h]
            rows = pl.ds(base + b0, 8)
            w_s[rows, :] = w_s[rows, :] * g
        return carry

    lax.fori_loop(0, te // PEER_N_KEYS, per_key, 0)
    acc_s[...] += jnp.dot(vt_ref[...], w_s[...].astype(BF16), preferred_element_type=F32)

    @pl.when(j == pl.num_programs(1) - 1)
    def _():
        f_ref[...] = acc_s[...].T


def _peer_dense(x1b, u_b, vt_b, r2t, nat, eat, ebt, tm, te):
    t = x1b.shape[0]
    n_exp = u_b.shape[0]
    ka = te // PEER_N_KEYS
    tok = pl.BlockSpec((PEER_HEADS, PEER_N_KEYS, tm), lambda i, j: (0, 0, i))
    key = pl.BlockSpec((PEER_HEADS, ka, tm), lambda i, j: (0, j, i))
    return pl.pallas_call(
        _peer_dense_kernel,
        grid=(t // tm, n_exp // te),
        in_specs=[
            pl.BlockSpec((tm, D_MODEL), lambda i, j: (i, 0)),
            pl.BlockSpec((te, D_MODEL), lambda i, j: (j, 0)),
            pl.BlockSpec((D_MODEL, te), lambda i, j: (0, j)),
            tok, tok, key, key,
        ],
        out_specs=pl.BlockSpec((tm, D_MODEL), lambda i, j: (i, 0)),
        out_shape=jax.ShapeDtypeStruct((t, D_MODEL), F32),
        scratch_shapes=[pltpu.VMEM((te, tm), F32), pltpu.VMEM((D_MODEL, tm), F32)],
        compiler_params=_cparams(("parallel", "arbitrary")),
        name="peer_dense",
    )(x1b, u_b, vt_b, r2t, ebt, nat, eat)


def _ln2_kernel(x1_ref, f_ref, g_ref, b_ref, o_ref, *, alpha):
    o_ref[...] = _layer_norm(alpha * x1_ref[...] + f_ref[...], g_ref[...], b_ref[...])


def _ln2(x1, f, g, b, tm, alpha):
    t = x1.shape[0]
    row = pl.BlockSpec((tm, D_MODEL), lambda i: (i, 0))
    vec = pl.BlockSpec((1, D_MODEL), lambda i: (0, 0))
    return pl.pallas_call(
        functools.partial(_ln2_kernel, alpha=alpha),
        grid=(t // tm,),
        in_specs=[row, row, vec, vec],
        out_specs=row,
        out_shape=jax.ShapeDtypeStruct((t, D_MODEL), F32),
        compiler_params=_cparams(("parallel",)),
        name="ln2",
    )(x1, f, g, b)


def _rope_tables(pos):
    half = DIFF_HEAD_DIM // 2
    inv_freq = ROPE_THETA ** (-jnp.arange(half, dtype=F32) / half)
    ang = pos.astype(F32)[:, None] * inv_freq[None, :]
    cos = jnp.cos(ang)
    sin = jnp.sin(ang)
    return jnp.tile(cos, (1, 4)), jnp.tile(jnp.concatenate([-sin, sin], axis=-1), (1, 2))


def _pick_tile(t, pref):
    tm = min(t, pref)
    assert t % tm == 0, (t, tm)
    return tm


def kernel(x_prompt, x_sample, cache_diff_k, cache_diff_v, cache_mla_ckv, cache_mla_krope, page_table, w_in, diff_lambda_q1, diff_lambda_k1, diff_lambda_q2, diff_lambda_k2, diff_subln_g, mla_q_norm_g, mla_kv_norm_g, w_uq, w_ukv, w_out, ln1_g, ln1_b, ln2_g, ln2_b, peer_wq, peer_q_norm_g, peer_subkeys, peer_u, peer_v):
    depth = w_in.shape[0]
    assert depth == 1, "single-layer trunk"
    layer = 0
    batch, seq, d = x_prompt.shape
    bd, dec_seq, _ = x_sample.shape
    assert d == D_MODEL and dec_seq == 1
    n_pool, page = cache_diff_k.shape[1], cache_diff_k.shape[2]
    past_len = page_table.shape[1] * page
    alpha = (2 * depth) ** 0.25
    lam_init = 0.8 - 0.6 * math.exp(-0.3 * layer)

    lam = (jnp.exp(jnp.sum(diff_lambda_q1[layer] * diff_lambda_k1[layer]))
           - jnp.exp(jnp.sum(diff_lambda_q2[layer] * diff_lambda_k2[layer])) + lam_init).reshape(1).astype(F32)

    w_in_l = w_in[layer]
    w_in_p = jnp.concatenate([w_in_l, w_in_l[:, OFF_KR:OFF_KR + MLA_ROPE_DIM]], axis=1).astype(BF16)
    w_uq_l = w_uq[layer]
    w_uq_p = jnp.concatenate([
        w_uq_l[:, :, :MLA_NOPE_DIM].reshape(MLA_Q_RANK, -1),
        w_uq_l[:, :, MLA_NOPE_DIM:].reshape(MLA_Q_RANK, -1)], axis=1).astype(BF16)
    w_ukv_l = w_ukv[layer]
    wn = w_ukv_l[:, :, :MLA_NOPE_DIM].transpose(1, 2, 0).astype(BF16)
    wv = w_ukv_l[:, :, MLA_NOPE_DIM:].transpose(1, 0, 2).astype(BF16)
    wo = w_out[layer].astype(BF16)
    wq = peer_wq[layer].astype(BF16)
    subkeys = peer_subkeys[layer].astype(BF16)
    u_b = peer_u[layer].astype(BF16)
    vt_b = peer_v[layer].T.astype(BF16)
    qg = mla_q_norm_g[layer].reshape(1, -1)
    kvg = mla_kv_norm_g[layer].reshape(1, -1)
    subln = diff_subln_g[layer].reshape(1, -1)
    pqg = peer_q_norm_g[layer].reshape(PEER_HEADS, 1, PEER_KEY_DIM)
    g1, b1 = ln1_g[layer].reshape(1, -1), ln1_b[layer].reshape(1, -1)
    g2, b2 = ln2_g[layer].reshape(1, -1), ln2_b[layer].reshape(1, -1)

    def project(x2d, pos, tm):
        cos, sin = _rope_tables(pos)
        return _proj(x2d, cos, sin, w_in_p, qg, w_uq_p, kvg, wn, tm)

    def block_out(x2d, od, oml, tm, tm_peer, te):
        x1, x1b = _out_ln1(x2d, od, oml, wv, wo, g1, b1, tm, alpha)
        r2t, nat, eat, ebt = _peer_query(x1b, wq, pqg, subkeys, tm)
        f = _peer_dense(x1b, u_b, vt_b, r2t, nat, eat, ebt, tm_peer, te)
        return _ln2(x1, f, g2, b2, tm, alpha)

    tp = batch * seq
    xp = x_prompt.reshape(tp, d)
    tm_p = _pick_tile(tp, 256)
    pos_p = jnp.tile(jnp.arange(seq, dtype=jnp.int32), batch)
    qd, kd, vd, ckv, kr, kdb, kcat, qlat, qrope, vdt, ckvt = project(xp, pos_p, tm_p)
    tq = _pick_tile(seq, 256)
    od = _diff_flash(lam, qd, kdb, vdt, subln.reshape(-1, 1), batch, seq, tq, lam_init)
    oml = _mla_flash(qlat, qrope, kcat, ckvt, batch, seq, tq)
    y_p = block_out(xp, od, oml, tm_p, _pick_tile(tp, 512), 1024).reshape(batch, seq, d)
    p_dk = kd.reshape(1, batch, seq, DIFF_KV_HEADS, 2, DIFF_HEAD_DIM)
    p_dv = vd.reshape(1, batch, seq, DIFF_KV_HEADS, DIFF_V_DIM)
    p_ckv = ckv.reshape(1, batch, seq, MLA_KV_RANK)
    p_kr = kr.reshape(1, batch, seq, MLA_ROPE_DIM)

    xs = x_sample.reshape(bd, d)
    tm_s = _pick_tile(bd, 128)
    pos_s = jnp.full((bd,), past_len, jnp.int32)
    qd, kd, vd, ckv, kr, _, _, qlat, qrope, _, _ = project(xs, pos_s, tm_s)
    q5 = qd.reshape(bd, DIFF_KV_HEADS, DIFF_GROUP, 2, DIFF_HEAD_DIM)
    zeros = jnp.zeros_like(q5[:, :, :, 0])
    qblk = jnp.concatenate([
        jnp.concatenate([q5[:, :, :, 0], zeros], axis=-1),
        jnp.concatenate([zeros, q5[:, :, :, 1]], axis=-1)], axis=2)
    od_s, oml_s = _decode(
        page_table, lam, qblk, kd.reshape(bd, 1, -1), vd.reshape(bd, 1, -1),
        qlat.reshape(bd, MLA_HEADS, MLA_KV_RANK), qrope.reshape(bd, MLA_HEADS, MLA_ROPE_DIM),
        ckv.reshape(bd, 1, -1), kr.reshape(bd, 1, -1), subln,
        cache_diff_k.transpose(0, 1, 3, 4, 5, 2).reshape(depth * n_pool, DIFF_K_COLS, page),
        cache_diff_v.reshape(depth * n_pool, page * DIFF_KV_HEADS, DIFF_V_DIM),
        cache_mla_ckv.reshape(depth * n_pool, page, MLA_KV_RANK),
        cache_mla_krope.transpose(0, 1, 3, 2).reshape(depth * n_pool, MLA_ROPE_DIM, page),
        pps=_pick_tile(page_table.shape[1], 32), lam_init=lam_init)
    y_s = block_out(xs, od_s, oml_s, tm_s, tm_s, 1024).reshape(bd, 1, d)
    s_dk = kd.reshape(1, bd, 1, DIFF_KV_HEADS, 2, DIFF_HEAD_DIM)
    s_dv = vd.reshape(1, bd, 1, DIFF_KV_HEADS, DIFF_V_DIM)
    s_ckv = ckv.reshape(1, bd, 1, MLA_KV_RANK)
    s_kr = kr.reshape(1, bd, 1, MLA_ROPE_DIM)

    return (y_p, y_s, p_dk, p_dv, p_ckv, p_kr, s_dk, s_dv, s_ckv, s_kr)
```

```python
import functools
import math

import jax
import jax.numpy as jnp
from jax import lax
from jax.experimental import pallas as pl
from jax.experimental.pallas import tpu as pltpu

BF16 = jnp.bfloat16
F32 = jnp.float32

D_MODEL = 2048
DIFF_HEAD_DIM = 64
DIFF_V_DIM = 128
DIFF_KV_HEADS = 2
DIFF_GROUP = 4
DIFF_SCALE = DIFF_HEAD_DIM ** -0.5
MLA_NOPE_DIM = 128
MLA_ROPE_DIM = 64
MLA_V_DIM = 128
MLA_HEADS = 8
MLA_Q_RANK = 512
MLA_KV_RANK = 256
MLA_SCALE = (MLA_NOPE_DIM + MLA_ROPE_DIM) ** -0.5
PEER_HEADS = 8
PEER_KEY_DIM = 256
PEER_N_KEYS = 128
PEER_TOPK = 16
PEER_CAND_ROWS = -(-sum(PEER_TOPK // (i + 1) for i in range(PEER_TOPK)) // 8) * 8
ROPE_THETA = 10000.0
LN_EPS = 1e-5
RMS_EPS = 1e-6

DIFF_Q_COLS = 1024
DIFF_K_COLS = 256
DIFF_V_COLS = 256
OFF_K = DIFF_Q_COLS
OFF_V = OFF_K + DIFF_K_COLS
OFF_CQ = OFF_V + DIFF_V_COLS
OFF_CKV = OFF_CQ + MLA_Q_RANK
OFF_KR = OFF_CKV + MLA_KV_RANK
D_IN_PAD = OFF_KR + 128

LANES = 128
ROWS_BF16 = 16
VMEM_LIMIT = 60 * 1024 * 1024

NT_DIMS = (((1,), (1,)), ((), ()))


def _cparams(sem):
    return pltpu.CompilerParams(dimension_semantics=sem, vmem_limit_bytes=VMEM_LIMIT)


def _rope_chunk(c, cos, sin_signed, first_half):
    sw = jnp.where(first_half, pltpu.roll(c, 96, 1), pltpu.roll(c, 32, 1))
    return c * cos + sw * sin_signed


def _proj_kernel(x_ref, cos_ref, sin_ref, win_ref, qg_ref, wuq_ref, kvg_ref, wn_ref,
                 qd_ref, kd_ref, vd_ref, ckv_ref, kr_ref, kdb_ref, kcat_ref, qlat_ref, qrope_ref,
                 vdt_ref, ckvt_ref):
    tm = x_ref.shape[0]
    z = jnp.dot(x_ref[...].astype(BF16), win_ref[...], preferred_element_type=F32)
    cos = cos_ref[...]
    sin = sin_ref[...]
    lane = lax.broadcasted_iota(jnp.int32, (tm, LANES), 1)
    first = (lane % 64) < 32

    def rope(c):
        return _rope_chunk(c, cos, sin, first)

    for j in range(DIFF_Q_COLS // LANES):
        qd_ref[:, j * LANES:(j + 1) * LANES] = rope(z[:, j * LANES:(j + 1) * LANES]).astype(BF16)
    for j in range(DIFF_K_COLS // LANES):
        k = rope(z[:, OFF_K + j * LANES:OFF_K + (j + 1) * LANES])
        kd_ref[:, j * LANES:(j + 1) * LANES] = k
        kdb_ref[:, j * LANES:(j + 1) * LANES] = k.astype(BF16)
    v = z[:, OFF_V:OFF_V + DIFF_V_COLS]
    vd_ref[...] = v
    vdt_ref[...] = v.T.astype(BF16)

    ckv = z[:, OFF_CKV:OFF_CKV + MLA_KV_RANK]
    ckv = ckv * lax.rsqrt(jnp.mean(ckv * ckv, axis=-1, keepdims=True) + RMS_EPS) * kvg_ref[...]
    ckv_ref[...] = ckv
    kcat_ref[:, :MLA_KV_RANK] = ckv.astype(BF16)
    ckvt_ref[...] = ckv.T.astype(BF16)
    kr2 = rope(z[:, OFF_KR:OFF_KR + LANES])
    kr_ref[...] = kr2[:, :MLA_ROPE_DIM]
    kcat_ref[:, MLA_KV_RANK:] = kr2.astype(BF16)

    cq = z[:, OFF_CQ:OFF_CQ + MLA_Q_RANK]
    cq = cq * lax.rsqrt(jnp.mean(cq * cq, axis=-1, keepdims=True) + RMS_EPS) * qg_ref[...]
    q = jnp.dot(cq.astype(BF16), wuq_ref[...], preferred_element_type=F32)
    nope_cols = MLA_HEADS * MLA_NOPE_DIM
    for j in range(MLA_HEADS * MLA_ROPE_DIM // LANES):
        qrope_ref[:, j * LANES:(j + 1) * LANES] = rope(
            q[:, nope_cols + j * LANES:nope_cols + (j + 1) * LANES]).astype(BF16)
    for h in range(MLA_HEADS):
        qn = q[:, h * MLA_NOPE_DIM:(h + 1) * MLA_NOPE_DIM].astype(BF16)
        qlat_ref[:, h * MLA_KV_RANK:(h + 1) * MLA_KV_RANK] = jnp.dot(
            qn, wn_ref[h], preferred_element_type=F32).astype(BF16)


def _proj(x2d, cos, sin, w_in_p, q_norm_g, w_uq_p, kv_norm_g, wn, tm):
    t = x2d.shape[0]
    row = lambda w: pl.BlockSpec((tm, w), lambda i: (i, 0))
    full = lambda a: pl.BlockSpec(a.shape, lambda i: (0,) * a.ndim)
    outs = [
        (DIFF_Q_COLS, BF16), (DIFF_K_COLS, F32), (DIFF_V_COLS, F32), (MLA_KV_RANK, F32),
        (MLA_ROPE_DIM, F32), (DIFF_K_COLS, BF16), (MLA_KV_RANK + LANES, BF16),
        (MLA_HEADS * MLA_KV_RANK, BF16), (MLA_HEADS * MLA_ROPE_DIM, BF16),
    ]
    col = lambda h: pl.BlockSpec((h, tm), lambda i: (0, i))
    return pl.pallas_call(
        _proj_kernel,
        grid=(t // tm,),
        in_specs=[row(D_MODEL), row(LANES), row(LANES), full(w_in_p), full(q_norm_g), full(w_uq_p),
                  full(kv_norm_g), full(wn)],
        out_specs=[row(w) for w, _ in outs] + [col(DIFF_V_COLS), col(MLA_KV_RANK)],
        out_shape=[jax.ShapeDtypeStruct((t, w), dt) for w, dt in outs]
        + [jax.ShapeDtypeStruct((DIFF_V_COLS, t), BF16), jax.ShapeDtypeStruct((MLA_KV_RANK, t), BF16)],
        compiler_params=_cparams(("parallel",)),
        name="proj",
    )(x2d, cos, sin, w_in_p, q_norm_g, w_uq_p, kv_norm_g, wn)


def _online_update(st, vt, m_s, l_s, acc_s, scale):
    c = scale * math.log2(math.e)
    m_prev = m_s[...]
    m_new = jnp.maximum(m_prev, jnp.max(st, axis=0, keepdims=True))
    corr = jnp.exp2((m_prev - m_new) * c)
    p = jnp.exp2((st - m_new) * c)
    l_s[...] = l_s[...] * corr + jnp.sum(p, axis=0, keepdims=True)
    acc_s[...] = acc_s[...] * corr + jnp.dot(vt, p.astype(BF16), preferred_element_type=F32)
    m_s[...] = m_new


def _causal_mask(st, tq):
    keys = lax.broadcasted_iota(jnp.int32, st.shape, 0)
    queries = lax.broadcasted_iota(jnp.int32, st.shape, 1) & (tq - 1)
    return jnp.where(keys <= queries, st, -jnp.inf)


def _causal_pairs(nq):
    pairs = [(i, j) for i in range(nq) for j in range(i + 1)]
    return (jnp.array([p[0] for p in pairs], jnp.int32), jnp.array([p[1] for p in pairs], jnp.int32))


def _diff_flash_kernel(qt_ref, kt_ref, lam_ref, q_ref, k_ref, v_ref, g_ref, o_ref, qs, m_s, l_s, acc_s,
                       *, tq, lam_init):
    qi = qt_ref[pl.program_id(2)]
    kj = kt_ref[pl.program_id(2)]

    @pl.when(kj == 0)
    def _():
        lane = lax.broadcasted_iota(jnp.int32, (tq, LANES), 1)
        zero = jnp.zeros((tq, LANES), BF16)
        for g in range(DIFF_GROUP):
            chunk = q_ref[:, g * LANES:(g + 1) * LANES] * DIFF_SCALE
            qs[g * tq:(g + 1) * tq, :] = jnp.where(lane < 64, chunk, zero)
            qs[(DIFF_GROUP + g) * tq:(DIFF_GROUP + g + 1) * tq, :] = jnp.where(lane >= 64, chunk, zero)
        m_s[...] = jnp.full(m_s.shape, -jnp.inf, F32)
        l_s[...] = jnp.zeros(l_s.shape, F32)
        acc_s[...] = jnp.zeros(acc_s.shape, F32)

    def scores():
        return lax.dot_general(k_ref[...], qs[...], NT_DIMS, preferred_element_type=F32)

    @pl.when(kj < qi)
    def _():
        _online_update(scores(), v_ref[...], m_s, l_s, acc_s, 1.0)

    @pl.when(kj == qi)
    def _():
        _online_update(_causal_mask(scores(), tq), v_ref[...], m_s, l_s, acc_s, 1.0)
        o = acc_s[...] / l_s[...]
        d = o[:, :DIFF_GROUP * tq] - lam_ref[0] * o[:, DIFF_GROUP * tq:]
        d = d * lax.rsqrt(jnp.mean(d * d, axis=0, keepdims=True) + RMS_EPS) * g_ref[...] * (1.0 - lam_init)
        for g in range(DIFF_GROUP):
            o_ref[:, g * DIFF_V_DIM:(g + 1) * DIFF_V_DIM] = d[:, g * tq:(g + 1) * tq].T.astype(BF16)


def _diff_flash(lam, qd, kdb, vdt, subln_col, batch, seq, tq, lam_init):
    nq = seq // tq
    ngc = 2 * DIFF_GROUP
    kern = functools.partial(_diff_flash_kernel, tq=tq, lam_init=lam_init)
    gw = DIFF_GROUP * 2 * DIFF_HEAD_DIM
    qt, kt = _causal_pairs(nq)
    return pl.pallas_call(
        kern,
        grid_spec=pltpu.PrefetchScalarGridSpec(
            num_scalar_prefetch=2,
            grid=(batch, DIFF_KV_HEADS, qt.shape[0]),
            in_specs=[
                pl.BlockSpec(memory_space=pltpu.SMEM),
                pl.BlockSpec((tq, gw), lambda b, h, t, qt, kt: (b * nq + qt[t], h)),
                pl.BlockSpec((tq, LANES), lambda b, h, t, qt, kt: (b * nq + kt[t], h)),
                pl.BlockSpec((DIFF_V_DIM, tq), lambda b, h, t, qt, kt: (h, b * nq + kt[t])),
                pl.BlockSpec((DIFF_V_DIM, 1), lambda b, h, t, qt, kt: (0, 0)),
            ],
            out_specs=pl.BlockSpec((tq, DIFF_GROUP * DIFF_V_DIM), lambda b, h, t, qt, kt: (b * nq + qt[t], h)),
            scratch_shapes=[
                pltpu.VMEM((ngc * tq, LANES), BF16),
                pltpu.VMEM((1, ngc * tq), F32),
                pltpu.VMEM((1, ngc * tq), F32),
                pltpu.VMEM((DIFF_V_DIM, ngc * tq), F32),
            ],
        ),
        out_shape=jax.ShapeDtypeStruct((batch * seq, DIFF_KV_HEADS * DIFF_GROUP * DIFF_V_DIM), BF16),
        compiler_params=_cparams(("parallel", "parallel", "arbitrary")),
        name="diff_flash",
    )(qt, kt, lam, qd, kdb, vdt, subln_col)


def _mla_flash_kernel(qt_ref, kt_ref, ql_ref, qr_ref, k_ref, vt_ref, o_ref, qs, m_s, l_s, acc_s, *, tq):
    qi = qt_ref[pl.program_id(1)]
    kj = kt_ref[pl.program_id(1)]

    @pl.when(kj == 0)
    def _():
        lane = lax.broadcasted_iota(jnp.int32, (tq, LANES), 1)
        zero = jnp.zeros((tq, LANES), BF16)
        for h in range(MLA_HEADS):
            qs[h * tq:(h + 1) * tq, :MLA_KV_RANK] = ql_ref[:, h * MLA_KV_RANK:(h + 1) * MLA_KV_RANK]
            chunk = qr_ref[:, (h // 2) * LANES:(h // 2 + 1) * LANES]
            keep = (lane < 64) if h % 2 == 0 else (lane >= 64)
            qs[h * tq:(h + 1) * tq, MLA_KV_RANK:] = jnp.where(keep, chunk, zero)
        m_s[...] = jnp.full(m_s.shape, -jnp.inf, F32)
        l_s[...] = jnp.zeros(l_s.shape, F32)
        acc_s[...] = jnp.zeros(acc_s.shape, F32)

    def scores():
        return lax.dot_general(k_ref[...], qs[...], NT_DIMS, preferred_element_type=F32)

    @pl.when(kj < qi)
    def _():
        _online_update(scores(), vt_ref[...], m_s, l_s, acc_s, MLA_SCALE)

    @pl.when(kj == qi)
    def _():
        _online_update(_causal_mask(scores(), tq), vt_ref[...], m_s, l_s, acc_s, MLA_SCALE)
        o = acc_s[...] / l_s[...]
        for h in range(MLA_HEADS):
            o_ref[:, h * MLA_KV_RANK:(h + 1) * MLA_KV_RANK] = o[:, h * tq:(h + 1) * tq].T.astype(BF16)


def _mla_flash(qlat, qrope, kcat, ckvt, batch, seq, tq):
    nq = seq // tq
    kern = functools.partial(_mla_flash_kernel, tq=tq)
    kw = MLA_KV_RANK + LANES
    qt, kt = _causal_pairs(nq)
    return pl.pallas_call(
        kern,
        grid_spec=pltpu.PrefetchScalarGridSpec(
            num_scalar_prefetch=2,
            grid=(batch, qt.shape[0]),
            in_specs=[
                pl.BlockSpec((tq, MLA_HEADS * MLA_KV_RANK), lambda b, t, qt, kt: (b * nq + qt[t], 0)),
                pl.BlockSpec((tq, MLA_HEADS * MLA_ROPE_DIM), lambda b, t, qt, kt: (b * nq + qt[t], 0)),
                pl.BlockSpec((tq, kw), lambda b, t, qt, kt: (b * nq + kt[t], 0)),
                pl.BlockSpec((MLA_KV_RANK, tq), lambda b, t, qt, kt: (0, b * nq + kt[t])),
            ],
            out_specs=pl.BlockSpec((tq, MLA_HEADS * MLA_KV_RANK), lambda b, t, qt, kt: (b * nq + qt[t], 0)),
            scratch_shapes=[
                pltpu.VMEM((MLA_HEADS * tq, kw), BF16),
                pltpu.VMEM((1, MLA_HEADS * tq), F32),
                pltpu.VMEM((1, MLA_HEADS * tq), F32),
                pltpu.VMEM((MLA_KV_RANK, MLA_HEADS * tq), F32),
            ],
        ),
        out_shape=jax.ShapeDtypeStruct((batch * seq, MLA_HEADS * MLA_KV_RANK), BF16),
        compiler_params=_cparams(("parallel", "arbitrary")),
        name="mla_flash",
    )(qt, kt, qlat, qrope, kcat, ckvt)


def _decode_kernel(pt_ref, lam_ref, qd_ref, kdn_ref, vdn_ref, ql_ref, qr_ref, ckvn_ref, krn_ref, g_ref,
                   dk_hbm, dv_hbm, ck_hbm, kr_hbm, od_ref, om_ref,
                   dk_buf, dv_buf, ck_buf, kr_buf, sems, md_s, ld_s, accd_s, mm_s, lm_s, accm_s,
                   *, pps, n_pages, lam_init):
    b = pl.program_id(0)
    n_steps = n_pages // pps
    page = ck_buf.shape[2]
    hbm_bufs =((dk_hbm, dk_buf), (dv_hbm, dv_buf), (ck_hbm, ck_buf), (kr_hbm, kr_buf))

    def page_copy(a, page_id, slot, p):
        hbm, buf = hbm_bufs[a]
        return pltpu.make_async_copy(hbm.at[page_id], buf.at[slot, p], sems.at[a, slot])

    def start_fetch(seq, step, slot):
        for p in range(pps):
            page_id = pt_ref[seq * n_pages + step * pps + p]
            for a in range(len(hbm_bufs)):
                page_copy(a, page_id, slot, p).start()

    def wait_fetch(slot):
        for p in range(pps):
            for a in range(len(hbm_bufs)):
                page_copy(a, 0, slot, p).wait()

    @pl.when(b == 0)
    def _():
        start_fetch(0, 0, 0)

    def init():
        for h in range(DIFF_KV_HEADS):
            q = qd_ref[0, h].astype(F32)
            kn = kdn_ref[0, :, h * LANES:(h + 1) * LANES]
            md_s[h] = jnp.sum(q * kn.astype(BF16).astype(F32), axis=-1, keepdims=True) * DIFF_SCALE
            ld_s[h] = jnp.ones((8, 1), F32)
            accd_s[h] = jnp.broadcast_to(vdn_ref[0, :, h * LANES:(h + 1) * LANES].astype(BF16).astype(F32),
                                         (8, DIFF_V_DIM))
        ql = ql_ref[0].astype(F32)
        qr = qr_ref[0].astype(F32)
        cn = ckvn_ref[0].astype(BF16).astype(F32)
        rn = krn_ref[0].astype(BF16).astype(F32)
        mm_s[...] = (jnp.sum(ql * cn, axis=-1, keepdims=True)
                     + jnp.sum(qr * rn, axis=-1, keepdims=True)) * MLA_SCALE
        lm_s[...] = jnp.ones((8, 1), F32)
        accm_s[...] = jnp.broadcast_to(cn, (8, MLA_KV_RANK))

    def merge(s_list, v_list, m_ref, l_ref, acc_ref):
        s = jnp.concatenate(s_list, axis=-1)
        m_prev = m_ref[...]
        m_new = jnp.maximum(m_prev, jnp.max(s, axis=-1, keepdims=True))
        corr = jnp.exp(m_prev - m_new)
        p = jnp.exp(s - m_new)
        l_ref[...] = l_ref[...] * corr + jnp.sum(p, axis=-1, keepdims=True)
        pb = p.astype(BF16)
        acc = acc_ref[...] * corr
        for i, v in enumerate(v_list):
            acc = acc + jnp.dot(pb[:, i * LANES:(i + 1) * LANES], v, preferred_element_type=F32)
        acc_ref[...] = acc
        m_ref[...] = m_new

    def compute(slot):
        for h in range(DIFF_KV_HEADS):
            q = qd_ref[0, h]
            s_list = [
                jnp.dot(q, dk_buf[slot, p, h * LANES:(h + 1) * LANES, :].astype(BF16),
                        preferred_element_type=F32) * DIFF_SCALE
                for p in range(pps)]
            vs = [dv_buf[slot, p, pl.ds(h, page, stride=DIFF_KV_HEADS), :].astype(BF16) for p in range(pps)]
            merge(s_list, vs, md_s.at[h], ld_s.at[h], accd_s.at[h])
        ql = ql_ref[0]
        qr = qr_ref[0]
        cs = [ck_buf[slot, p].astype(BF16) for p in range(pps)]
        s_list = [
            (lax.dot_general(ql, cs[p], NT_DIMS, preferred_element_type=F32)
             + jnp.dot(qr, kr_buf[slot, p].astype(BF16), preferred_element_type=F32)) * MLA_SCALE
            for p in range(pps)]
        merge(s_list, cs, mm_s, lm_s, accm_s)

    def body(step, carry):
        slot = (b * n_steps + step) & 1

        @pl.when(step + 1 < n_steps)
        def _():
            start_fetch(b, step + 1, 1 - slot)

        @pl.when(jnp.logical_and(step + 1 == n_steps, b + 1 < pl.num_programs(0)))
        def _():
            start_fetch(b + 1, 0, 1 - slot)

        wait_fetch(slot)
        compute(slot)
        return carry

    init()
    lax.fori_loop(0, n_steps, body, 0)

    def finish():
        for h in range(DIFF_KV_HEADS):
            o = accd_s[h] / ld_s[h]
            d = o[:DIFF_GROUP] - lam_ref[0] * o[DIFF_GROUP:]
            d = d * lax.rsqrt(jnp.mean(d * d, axis=-1, keepdims=True) + RMS_EPS) * g_ref[...] * (1.0 - lam_init)
            for g in range(DIFF_GROUP):
                c0 = (h * DIFF_GROUP + g) * DIFF_V_DIM
                od_ref[0, :, c0:c0 + DIFF_V_DIM] = d[g:g + 1].astype(BF16)
        o = accm_s[...] / lm_s[...]
        for h in range(MLA_HEADS):
            om_ref[0, :, h * MLA_KV_RANK:(h + 1) * MLA_KV_RANK] = o[h:h + 1].astype(BF16)

    finish()


def _decode(page_table, lam, qblk, kd_new, vd_new, ql, qr, ckv_new, kr_new, subln_g,
            cache_dk, cache_dv, cache_ckv, cache_kr, pps, lam_init):
    bd, n_pages = page_table.shape
    pt_flat = page_table.reshape(-1)
    per_seq = lambda shape: pl.BlockSpec((1,) + shape, lambda b, pt: (b,) + (0,) * len(shape))
    hbm = pl.BlockSpec(memory_space=pl.ANY)
    in_specs = [
        pl.BlockSpec(memory_space=pltpu.SMEM),
        per_seq((DIFF_KV_HEADS, 8, LANES)),
        per_seq((1, DIFF_K_COLS)), per_seq((1, DIFF_V_COLS)),
        per_seq((MLA_HEADS, MLA_KV_RANK)), per_seq((MLA_HEADS, MLA_ROPE_DIM)),
        per_seq((1, MLA_KV_RANK)), per_seq((1, MLA_ROPE_DIM)),
        pl.BlockSpec((1, DIFF_V_DIM), lambda b, pt: (0, 0)),
        hbm, hbm, hbm, hbm,
    ]
    caches = (cache_dk, cache_dv, cache_ckv, cache_kr)
    kern = functools.partial(_decode_kernel, pps=pps, n_pages=n_pages, lam_init=lam_init)
    od, om = pl.pallas_call(
        kern,
        grid_spec=pltpu.PrefetchScalarGridSpec(
            num_scalar_prefetch=1,
            grid=(bd,),
            in_specs=in_specs,
            out_specs=[per_seq((1, DIFF_KV_HEADS * DIFF_GROUP * DIFF_V_DIM)),
                       per_seq((1, MLA_HEADS * MLA_KV_RANK))],
            scratch_shapes=[pltpu.VMEM((2, pps) + c.shape[1:], c.dtype) for c in caches] + [
                pltpu.SemaphoreType.DMA((len(caches), 2)),
                pltpu.VMEM((DIFF_KV_HEADS, 8, 1), F32), pltpu.VMEM((DIFF_KV_HEADS, 8, 1), F32),
                pltpu.VMEM((DIFF_KV_HEADS, 8, DIFF_V_DIM), F32),
                pltpu.VMEM((8, 1), F32), pltpu.VMEM((8, 1), F32), pltpu.VMEM((8, MLA_KV_RANK), F32),
            ],
        ),
        out_shape=[jax.ShapeDtypeStruct((bd, 1, DIFF_KV_HEADS * DIFF_GROUP * DIFF_V_DIM), BF16),
                   jax.ShapeDtypeStruct((bd, 1, MLA_HEADS * MLA_KV_RANK), BF16)],
        compiler_params=_cparams(("arbitrary",)),
        name="decode",
    )(pt_flat, lam, qblk, kd_new, vd_new, ql, qr, ckv_new, kr_new, subln_g, *caches)
    return od.reshape(bd, -1), om.reshape(bd, -1)


def _layer_norm(y, g, b):
    yc = y - jnp.mean(y, axis=-1, keepdims=True)
    var = jnp.mean(yc * yc, axis=-1, keepdims=True)
    return yc * lax.rsqrt(var + LN_EPS) * g + b


def _out_ln1_kernel(x_ref, od_ref, oml_ref, wv_ref, wo_ref, g_ref, b_ref, x1_ref, x1b_ref, mix_s, *, alpha):
    half = od_ref.shape[1]
    mix_s[:, :half] = od_ref[...]
    for h in range(MLA_HEADS):
        om = jnp.dot(oml_ref[:, h * MLA_KV_RANK:(h + 1) * MLA_KV_RANK], wv_ref[h], preferred_element_type=F32)
        mix_s[:, half + h * MLA_V_DIM:half + (h + 1) * MLA_V_DIM] = om.astype(BF16)
    y = alpha * x_ref[...] + jnp.dot(mix_s[...], wo_ref[...], preferred_element_type=F32)
    x1 = _layer_norm(y, g_ref[...], b_ref[...])
    x1_ref[...] = x1
    x1b_ref[...] = x1.astype(BF16)


def _out_ln1(x2d, od, oml, wv, wo, g, b, tm, alpha):
    t = x2d.shape[0]
    row = lambda w: pl.BlockSpec((tm, w), lambda i: (i, 0))
    full = lambda a: pl.BlockSpec(a.shape, lambda i: (0,) * a.ndim)
    return pl.pallas_call(
        functools.partial(_out_ln1_kernel, alpha=alpha),
        grid=(t // tm,),
        in_specs=[row(D_MODEL), row(od.shape[1]), row(oml.shape[1]), full(wv), full(wo), full(g), full(b)],
        out_specs=[row(D_MODEL), row(D_MODEL)],
        out_shape=[jax.ShapeDtypeStruct((t, D_MODEL), F32), jax.ShapeDtypeStruct((t, D_MODEL), BF16)],
        scratch_shapes=[pltpu.VMEM((tm, D_MODEL), BF16)],
        compiler_params=_cparams(("parallel",)),
        name="out_ln1",
    )(x2d, od, oml, wv, wo, g, b)


def _peer_query_kernel(x1b_ref, wq_ref, qg_ref, sk_ref, r2_ref, na_ref, ea_ref, eb_ref,
                       t1_s, t2_s, cand_s, s_s):
    tm = x1b_ref.shape[0]
    q = jnp.dot(x1b_ref[...], wq_ref[...], preferred_element_type=F32)
    half = PEER_KEY_DIM // 2
    neg = -jnp.inf

    def top_values(s, t_s, want_rank):
        work = s
        rank = jnp.full(s.shape, float(PEER_TOPK), F32) if want_rank else None
        for i in range(PEER_TOPK):
            m = jnp.max(work, axis=0, keepdims=True)
            t_s[i:i + 1, :] = m
            hit = work == m
            if want_rank:
                rank = jnp.where(hit, float(i), rank)
            work = jnp.where(hit, neg, work)
        return rank

    lw = t1_s.shape[1]

    def select(h, lanes):
        s1 = s_s[0, :, lanes]
        s2 = s_s[1, :, lanes]
        top_values(s1, t1_s, False)
        rank2 = top_values(s2, t2_s, True)
        t2 = t2_s[...]
        row0 = 0
        for i in range(PEER_TOPK):
            n = PEER_TOPK // (i + 1)
            cand_s[row0:row0 + n, :] = t1_s[i:i + 1, :] + t2_s[0:n, :]
            row0 += n
        cand_s[row0:, :] = jnp.full((cand_s.shape[0] - row0, lw), neg, F32)
        work = cand_s[...]
        thr = None
        for i in range(PEER_TOPK):
            thr = jnp.max(work, axis=0, keepdims=True)
            work = jnp.where(work == thr, neg, work)
        top = t1_s[0:1, :] + t2_s[0:1, :]
        cand = cand_s[...]
        z = jnp.sum(jnp.where(cand >= thr, jnp.exp(cand - top), 0.0), axis=0, keepdims=True)
        na = jnp.zeros((PEER_N_KEYS, lw), F32)
        for i in range(PEER_TOPK):
            t1i = t1_s[i:i + 1, :]
            ni = jnp.sum(jnp.where(t1i + t2 >= thr, 1.0, 0.0), axis=0, keepdims=True)
            na = jnp.where(s1 == t1i, ni, na)
        r2_ref[h, :, lanes] = rank2
        na_ref[h, :, lanes] = na
        ea_ref[h, :, lanes] = jnp.exp(s1 - t1_s[0:1, :]) / z
        eb_ref[h, :, lanes] = jnp.exp(s2 - t2_s[0:1, :])

    for h in range(PEER_HEADS):
        qh = q[:, h * PEER_KEY_DIM:(h + 1) * PEER_KEY_DIM]
        qn = (qh * lax.rsqrt(jnp.mean(qh * qh, axis=-1, keepdims=True) + RMS_EPS) * qg_ref[h]).astype(BF16)
        s_s[0] = lax.dot_general(sk_ref[h, 0], qn[:, :half], NT_DIMS, preferred_element_type=F32)
        s_s[1] = lax.dot_general(sk_ref[h, 1], qn[:, half:], NT_DIMS, preferred_element_type=F32)
        for lt in range(tm // lw):
            select(h, slice(lt * lw, (lt + 1) * lw))


def _peer_query(x1b, wq, qg, subkeys, tm):
    t = x1b.shape[0]
    full = lambda a: pl.BlockSpec(a.shape, lambda i: (0,) * a.ndim)
    kt = pl.BlockSpec((PEER_HEADS, PEER_N_KEYS, tm), lambda i: (0, 0, i))
    shp = jax.ShapeDtypeStruct((PEER_HEADS, PEER_N_KEYS, t), F32)
    lw = min(tm, LANES)
    return pl.pallas_call(
        _peer_query_kernel,
        grid=(t // tm,),
        in_specs=[pl.BlockSpec((tm, D_MODEL), lambda i: (i, 0)), full(wq), full(qg), full(subkeys)],
        out_specs=[kt, kt, kt, kt],
        out_shape=[shp, shp, shp, shp],
        scratch_shapes=[pltpu.VMEM((PEER_TOPK, lw), F32), pltpu.VMEM((PEER_TOPK, lw), F32),
                        pltpu.VMEM((PEER_CAND_ROWS, lw), F32), pltpu.VMEM((2, PEER_N_KEYS, tm), F32)],
        compiler_params=_cparams(("parallel",)),
        name="peer_query",
    )(x1b, wq, qg, subkeys)


def _gelu(x):
    return 0.5 * x * (1.0 + lax.erf(x * (2.0 ** -0.5)))


def _peer_dense_kernel(x1b_ref, u_ref, vt_ref, r2_ref, eb_ref, na_ref, ea_ref, x1_ref, g_ref, b_ref, o_ref,
                       w_s, acc_s, *, alpha):
    j = pl.program_id(1)
    te = u_ref.shape[0]
    tm = x1b_ref.shape[0]

    @pl.when(j == 0)
    def _():
        acc_s[...] = jnp.zeros(acc_s.shape, F32)

    ht = lax.dot_general(u_ref[...], x1b_ref[...], NT_DIMS, preferred_element_type=F32)
    w_s[...] = _gelu(ht)

    def per_key(k, carry):
        na = [na_ref[h, pl.ds(k, 1), :] for h in range(PEER_HEADS)]
        ea = [ea_ref[h, pl.ds(k, 1), :] for h in range(PEER_HEADS)]
        base = pl.multiple_of(k * PEER_N_KEYS, PEER_N_KEYS)
        for b0 in range(0, PEER_N_KEYS, 8):
            g = jnp.zeros((8, tm), F32)
            for h in range(PEER_HEADS):
                sel = r2_ref[h, b0:b0 + 8, :] < na[h]
                g = g + jnp.where(sel, eb_ref[h, b0:b0 + 8, :], 0.0) * ea[h]
            rows = pl.ds(base + b0, 8)
            w_s[rows, :] = w_s[rows, :] * g
        return carry

    lax.fori_loop(0, te // PEER_N_KEYS, per_key, 0)
    acc_s[...] += jnp.dot(vt_ref[...], w_s[...].astype(BF16), preferred_element_type=F32)

    @pl.when(j == pl.num_programs(1) - 1)
    def _():
        o_ref[...] = _layer_norm(alpha * x1_ref[...] + acc_s[...].T, g_ref[...], b_ref[...])


def _peer_dense(x1b, u_b, vt_b, r2t, nat, eat, ebt, x1, g2, b2, tm, te, alpha):
    t = x1b.shape[0]
    n_exp = u_b.shape[0]
    ka = te // PEER_N_KEYS
    tok = pl.BlockSpec((PEER_HEADS, PEER_N_KEYS, tm), lambda i, j: (0, 0, i))
    key = pl.BlockSpec((PEER_HEADS, ka, tm), lambda i, j: (0, j, i))
    row = pl.BlockSpec((tm, D_MODEL), lambda i, j: (i, 0))
    vec = pl.BlockSpec((1, D_MODEL), lambda i, j: (0, 0))
    return pl.pallas_call(
        functools.partial(_peer_dense_kernel, alpha=alpha),
        grid=(t // tm, n_exp // te),
        in_specs=[
            row,
            pl.BlockSpec((te, D_MODEL), lambda i, j: (j, 0)),
            pl.BlockSpec((D_MODEL, te), lambda i, j: (0, j)),
            tok, tok, key, key,
            row, vec, vec,
        ],
        out_specs=row,
        out_shape=jax.ShapeDtypeStruct((t, D_MODEL), F32),
        scratch_shapes=[pltpu.VMEM((te, tm), F32), pltpu.VMEM((D_MODEL, tm), F32)],
        compiler_params=_cparams(("parallel", "arbitrary")),
        name="peer_dense",
    )(x1b, u_b, vt_b, r2t, ebt, nat, eat, x1, g2, b2)


def _rope_tables(pos):
    half = DIFF_HEAD_DIM // 2
    inv_freq = ROPE_THETA ** (-jnp.arange(half, dtype=F32) / half)
    ang = pos.astype(F32)[:, None] * inv_freq[None, :]
    cos = jnp.cos(ang)
    sin = jnp.sin(ang)
    return jnp.tile(cos, (1, 4)), jnp.tile(jnp.concatenate([-sin, sin], axis=-1), (1, 2))


def _pick_tile(t, pref):
    tm = min(t, pref)
    assert t % tm == 0, (t, tm)
    return tm


def kernel(x_prompt, x_sample, cache_diff_k, cache_diff_v, cache_mla_ckv, cache_mla_krope, page_table, w_in, diff_lambda_q1, diff_lambda_k1, diff_lambda_q2, diff_lambda_k2, diff_subln_g, mla_q_norm_g, mla_kv_norm_g, w_uq, w_ukv, w_out, ln1_g, ln1_b, ln2_g, ln2_b, peer_wq, peer_q_norm_g, peer_subkeys, peer_u, peer_v):
    depth = w_in.shape[0]
    assert depth == 1, "single-layer trunk"
    layer = 0
    batch, seq, d = x_prompt.shape
    bd, dec_seq, _ = x_sample.shape
    assert d == D_MODEL and dec_seq == 1
    n_pool, page = cache_diff_k.shape[1], cache_diff_k.shape[2]
    past_len = page_table.shape[1] * page
    alpha = (2 * depth) ** 0.25
    lam_init = 0.8 - 0.6 * math.exp(-0.3 * layer)

    lam = (jnp.exp(jnp.sum(diff_lambda_q1[layer] * diff_lambda_k1[layer]))
           - jnp.exp(jnp.sum(diff_lambda_q2[layer] * diff_lambda_k2[layer])) + lam_init).reshape(1).astype(F32)

    w_in_l = w_in[layer]
    w_in_p = jnp.concatenate([w_in_l, w_in_l[:, OFF_KR:OFF_KR + MLA_ROPE_DIM]], axis=1).astype(BF16)
    w_uq_l = w_uq[layer]
    w_uq_p = jnp.concatenate([
        w_uq_l[:, :, :MLA_NOPE_DIM].reshape(MLA_Q_RANK, -1),
        w_uq_l[:, :, MLA_NOPE_DIM:].reshape(MLA_Q_RANK, -1)], axis=1).astype(BF16)
    w_ukv_l = w_ukv[layer]
    wn = w_ukv_l[:, :, :MLA_NOPE_DIM].transpose(1, 2, 0).astype(BF16)
    wv = w_ukv_l[:, :, MLA_NOPE_DIM:].transpose(1, 0, 2).astype(BF16)
    wo = w_out[layer].astype(BF16)
    wq = peer_wq[layer].astype(BF16)
    subkeys = peer_subkeys[layer].astype(BF16)
    u_b = peer_u[layer].astype(BF16)
    vt_b = peer_v[layer].T.astype(BF16)
    qg = mla_q_norm_g[layer].reshape(1, -1)
    kvg = mla_kv_norm_g[layer].reshape(1, -1)
    subln = diff_subln_g[layer].reshape(1, -1)
    pqg = peer_q_norm_g[layer].reshape(PEER_HEADS, 1, PEER_KEY_DIM)
    g1, b1 = ln1_g[layer].reshape(1, -1), ln1_b[layer].reshape(1, -1)
    g2, b2 = ln2_g[layer].reshape(1, -1), ln2_b[layer].reshape(1, -1)

    def project(x2d, pos, tm):
        cos, sin = _rope_tables(pos)
        return _proj(x2d, cos, sin, w_in_p, qg, w_uq_p, kvg, wn, tm)

    def block_out(x2d, od, oml, tm, tm_peer, te):
        x1, x1b = _out_ln1(x2d, od, oml, wv, wo, g1, b1, tm, alpha)
        r2t, nat, eat, ebt = _peer_query(x1b, wq, pqg, subkeys, tm)
        return _peer_dense(x1b, u_b, vt_b, r2t, nat, eat, ebt, x1, g2, b2, tm_peer, te, alpha)

    tp = batch * seq
    xp = x_prompt.reshape(tp, d)
    tm_p = _pick_tile(tp, 256)
    pos_p = jnp.tile(jnp.arange(seq, dtype=jnp.int32), batch)
    qd, kd, vd, ckv, kr, kdb, kcat, qlat, qrope, vdt, ckvt = project(xp, pos_p, tm_p)
    tq = _pick_tile(seq, 256)
    od = _diff_flash(lam, qd, kdb, vdt, subln.reshape(-1, 1), batch, seq, tq, lam_init)
    oml = _mla_flash(qlat, qrope, kcat, ckvt, batch, seq, tq)
    y_p = block_out(xp, od, oml, tm_p, _pick_tile(tp, 512), 1024).reshape(batch, seq, d)
    p_dk = kd.reshape(1, batch, seq, DIFF_KV_HEADS, 2, DIFF_HEAD_DIM)
    p_dv = vd.reshape(1, batch, seq, DIFF_KV_HEADS, DIFF_V_DIM)
    p_ckv = ckv.reshape(1, batch, seq, MLA_KV_RANK)
    p_kr = kr.reshape(1, batch, seq, MLA_ROPE_DIM)

    xs = x_sample.reshape(bd, d)
    tm_s = _pick_tile(bd, 128)
    pos_s = jnp.full((bd,), past_len, jnp.int32)
    qd, kd, vd, ckv, kr, _, _, qlat, qrope, _, _ = project(xs, pos_s, tm_s)
    q5 = qd.reshape(bd, DIFF_KV_HEADS, DIFF_GROUP, 2, DIFF_HEAD_DIM)
    zeros = jnp.zeros_like(q5[:, :, :, 0])
    qblk = jnp.concatenate([
        jnp.concatenate([q5[:, :, :, 0], zeros], axis=-1),
        jnp.concatenate([zeros, q5[:, :, :, 1]], axis=-1)], axis=2)
    od_s, oml_s = _decode(
        page_table, lam, qblk, kd.reshape(bd, 1, -1), vd.reshape(bd, 1, -1),
        qlat.reshape(bd, MLA_HEADS, MLA_KV_RANK), qrope.reshape(bd, MLA_HEADS, MLA_ROPE_DIM),
        ckv.reshape(bd, 1, -1), kr.reshape(bd, 1, -1), subln,
        cache_diff_k.transpose(0, 1, 3, 4, 5, 2).reshape(depth * n_pool, DIFF_K_COLS, page),
        cache_diff_v.reshape(depth * n_pool, page * DIFF_KV_HEADS, DIFF_V_DIM),
        cache_mla_ckv.reshape(depth * n_pool, page, MLA_KV_RANK),
        cache_mla_krope.transpose(0, 1, 3, 2).reshape(depth * n_pool, MLA_ROPE_DIM, page),
        pps=_pick_tile(page_table.shape[1], 32), lam_init=lam_init)
    y_s = block_out(xs, od_s, oml_s, tm_s, tm_s, 1024).reshape(bd, 1, d)
    s_dk = kd.reshape(1, bd, 1, DIFF_KV_HEADS, 2, DIFF_HEAD_DIM)
    s_dv = vd.reshape(1, bd, 1, DIFF_KV_HEADS, DIFF_V_DIM)
    s_ckv = ckv.reshape(1, bd, 1, MLA_KV_RANK)
    s_kr = kr.reshape(1, bd, 1, MLA_ROPE_DIM)

    return (y_p, y_s, p_dk, p_dv, p_ckv, p_kr, s_dk, s_dv, s_ckv, s_kr)
```
